```python
import math
import jax, jax.numpy as jnp
from jax import lax
import numpy as np

D_MODEL = 1024
BATCH = 1
SEQ = 16384
DEPTH = 1

ATTN_WIDTH = D_MODEL // 2
SSM_WIDTH = D_MODEL - ATTN_WIDTH
HEAD_DIM = 64
N_DIFF_HEADS = ATTN_WIDTH // (2 * HEAD_DIM)
V_DIM = 2 * HEAD_DIM
QK_WIDTH = N_DIFF_HEADS * 2 * HEAD_DIM
V_WIDTH = N_DIFF_HEADS * V_DIM
ROT_DIM = HEAD_DIM // 4
ROPE_THETA = 500000.0
Q_BLOCK = 128
SSM_GROUP = 16
N_SSM_GROUPS = SSM_WIDTH // SSM_GROUP
SSM_STATE = 64
DT_MIN = 1e-3
DT_MAX = 1e-1
IN_WIDTH = 2 * QK_WIDTH + V_WIDTH + SSM_WIDTH
D_FF = ((-(-8 * D_MODEL // 3) + 255) // 256) * 256
NORM_EPS = 1e-5

kernel_name = "hymba_s5_diffattn_layer"


def rms_norm(x, g):
    x32 = x.astype(jnp.float32)
    y = x32 * lax.rsqrt(jnp.mean(x32 * x32, axis=-1, keepdims=True) + NORM_EPS)
    return (y * g.astype(jnp.float32)).astype(x.dtype)


def rope_tables(positions):
    inv_freq = ROPE_THETA ** (-jnp.arange(0, ROT_DIM, 2, dtype=jnp.float32) / ROT_DIM)
    ang = positions.astype(jnp.float32)[..., None] * inv_freq
    ang = jnp.concatenate([ang, ang], axis=-1)
    return jnp.cos(ang), jnp.sin(ang)


def partial_rope(t, cos, sin):
    half = ROT_DIM // 2
    rot = t[..., :ROT_DIM].astype(jnp.float32)
    c = cos[:, :, None, None, :]
    s = sin[:, :, None, None, :]
    rotated = rot * c + jnp.concatenate([-rot[..., half:], rot[..., :half]], axis=-1) * s
    return jnp.concatenate([rotated.astype(t.dtype), t[..., ROT_DIM:]], axis=-1)


def diff_attention(q, k, v, lam):
    b, s_len, h, e = v.shape
    n_blocks = s_len // Q_BLOCK
    scale = HEAD_DIM ** -0.5
    k_pos = jnp.arange(s_len)

    def block(i):
        start = i * Q_BLOCK
        qb = lax.dynamic_slice_in_dim(q, start, Q_BLOCK, axis=1)
        sc = jnp.einsum('bqhcd,bkhcd->bhcqk', qb, k).astype(jnp.float32) * scale
        q_pos = start + jnp.arange(Q_BLOCK)
        causal = k_pos[None, :] <= q_pos[:, None]
        sc = jnp.where(causal, sc, -jnp.inf)
        p = jax.nn.softmax(sc, axis=-1)
        a = p[:, :, 0] - lam * p[:, :, 1]
        return jnp.einsum('bhqk,bkhe->bqhe', a.astype(v.dtype), v)

    out = lax.map(block, jnp.arange(n_blocks))
    return jnp.moveaxis(out, 0, 1).reshape(b, s_len, h, e)


def s5_mixer(u, lam_re, lam_im, log_step, b_re, b_im, c_re, c_im, d, w_glu, b_glu):
    bsz, s_len, _ = u.shape
    f32 = jnp.float32
    u32 = u.astype(f32).reshape(bsz, s_len, N_SSM_GROUPS, SSM_GROUP)
    lr = jnp.minimum(lam_re.astype(f32), -1e-4)
    li = lam_im.astype(f32)
    step = jnp.exp(log_step.astype(f32))[:, None]
    mag = jnp.exp(lr * step)
    lb_re = mag * jnp.cos(li * step)
    lb_im = mag * jnp.sin(li * step)
    denom = lr * lr + li * li
    nr = lb_re - 1.0
    ni = lb_im
    coef_re = (nr * lr + ni * li) / denom
    coef_im = (ni * lr - nr * li) / denom
    br = b_re.astype(f32)
    bi = b_im.astype(f32)
    bb_re = coef_re[..., None] * br - coef_im[..., None] * bi
    bb_im = coef_re[..., None] * bi + coef_im[..., None] * br
    bu_re = jnp.einsum('gph,bsgh->bsgp', bb_re, u32)
    bu_im = jnp.einsum('gph,bsgh->bsgp', bb_im, u32)
    a_re = jnp.broadcast_to(lb_re, bu_re.shape)
    a_im = jnp.broadcast_to(lb_im, bu_im.shape)

    def combine(e1, e2):
        a1r, a1i, b1r, b1i = e1
        a2r, a2i, b2r, b2i = e2
        return (a2r * a1r - a2i * a1i,
                a2r * a1i + a2i * a1r,
                a2r * b1r - a2i * b1i + b2r,
                a2r * b1i + a2i * b1r + b2i)

    _, _, x_re, x_im = lax.associative_scan(combine, (a_re, a_im, bu_re, bu_im), axis=1)
    y = (jnp.einsum('ghp,bsgp->bsgh', c_re.astype(f32), x_re)
         - jnp.einsum('ghp,bsgp->bsgh', c_im.astype(f32), x_im)
         + d.astype(f32) * u32)
    y = jax.nn.gelu(y.reshape(bsz, s_len, SSM_WIDTH))
    y = y * jax.nn.sigmoid(y @ w_glu.astype(f32) + b_glu.astype(f32))
    return y.astype(u.dtype)


def setup_inputs(seed: int = 0) -> dict:
    key = jax.random.key(seed)
    ks = jax.random.split(key, 24)
    f32 = jnp.float32
    L, G, P, H = DEPTH, N_SSM_GROUPS, SSM_STATE, SSM_GROUP

    def nrm(k, shape, scale):
        return jax.random.normal(k, shape, f32) * scale

    x = jax.random.normal(ks[0], (BATCH, SEQ, D_MODEL), f32)
    positions = jnp.broadcast_to(jnp.arange(SEQ, dtype=jnp.int32), (BATCH, SEQ))
    norm1_g = 1.0 + nrm(ks[1], (L, D_MODEL), 0.02)
    w_in = nrm(ks[2], (L, D_MODEL, IN_WIDTH), D_MODEL ** -0.5)
    lambda_q1 = nrm(ks[3], (L, HEAD_DIM), 0.1)
    lambda_k1 = nrm(ks[4], (L, HEAD_DIM), 0.1)
    lambda_q2 = nrm(ks[5], (L, HEAD_DIM), 0.1)
    lambda_k2 = nrm(ks[6], (L, HEAD_DIM), 0.1)
    subln_g = 1.0 + nrm(ks[7], (L, V_DIM), 0.02)
    ssm_lambda_re = -0.5 + nrm(ks[8], (L, G, P), 0.01)
    ssm_lambda_im = math.pi * jnp.arange(P, dtype=f32) + nrm(ks[9], (L, G, P), 0.01)
    ssm_log_step = jax.random.uniform(ks[10], (L, G), f32, math.log(DT_MIN), math.log(DT_MAX))
    ssm_b_re = nrm(ks[11], (L, G, P, H), (2 * H) ** -0.5)
    ssm_b_im = nrm(ks[12], (L, G, P, H), (2 * H) ** -0.5)
    ssm_c_re = nrm(ks[13], (L, G, H, P), P ** -0.5)
    ssm_c_im = nrm(ks[14], (L, G, H, P), P ** -0.5)
    ssm_d = nrm(ks[15], (L, G, H), 1.0)
    ssm_w_glu = nrm(ks[16], (L, SSM_WIDTH, SSM_WIDTH), SSM_WIDTH ** -0.5)
    ssm_b_glu = nrm(ks[17], (L, SSM_WIDTH), 0.01)
    w_out = nrm(ks[18], (L, D_MODEL, D_MODEL), D_MODEL ** -0.5)
    norm2_g = 1.0 + nrm(ks[19], (L, D_MODEL), 0.02)
    w_gate = nrm(ks[20], (L, D_MODEL, D_FF), D_MODEL ** -0.5)
    w_up = nrm(ks[21], (L, D_MODEL, D_FF), D_MODEL ** -0.5)
    w_down = nrm(ks[22], (L, D_FF, D_MODEL), D_FF ** -0.5)
    final_g = 1.0 + nrm(ks[23], (D_MODEL,), 0.02)
    return {"x": x, "positions": positions, "norm1_g": norm1_g, "w_in": w_in,
            "lambda_q1": lambda_q1, "lambda_k1": lambda_k1, "lambda_q2": lambda_q2,
            "lambda_k2": lambda_k2, "subln_g": subln_g,
            "ssm_lambda_re": ssm_lambda_re, "ssm_lambda_im": ssm_lambda_im,
            "ssm_log_step": ssm_log_step, "ssm_b_re": ssm_b_re, "ssm_b_im": ssm_b_im,
            "ssm_c_re": ssm_c_re, "ssm_c_im": ssm_c_im, "ssm_d": ssm_d,
            "ssm_w_glu": ssm_w_glu, "ssm_b_glu": ssm_b_glu, "w_out": w_out,
            "norm2_g": norm2_g, "w_gate": w_gate, "w_up": w_up, "w_down": w_down,
            "final_g": final_g}


def reference(x, positions, norm1_g, w_in, lambda_q1, lambda_k1, lambda_q2, lambda_k2,
              subln_g, ssm_lambda_re, ssm_lambda_im, ssm_log_step, ssm_b_re, ssm_b_im,
              ssm_c_re, ssm_c_im, ssm_d, ssm_w_glu, ssm_b_glu, w_out, norm2_g,
              w_gate, w_up, w_down, final_g):
    bsz, s_len, _ = x.shape
    cos, sin = rope_tables(positions)
    h = x
    for l in range(DEPTH):
        lambda_init = 0.8 - 0.6 * math.exp(-0.3 * l)
        hn = rms_norm(h, norm1_g[l])
        proj = hn @ w_in[l]
        q, k, v, u = jnp.split(proj, [QK_WIDTH, 2 * QK_WIDTH, 2 * QK_WIDTH + V_WIDTH], axis=-1)
        q = partial_rope(q.reshape(bsz, s_len, N_DIFF_HEADS, 2, HEAD_DIM), cos, sin)
        k = partial_rope(k.reshape(bsz, s_len, N_DIFF_HEADS, 2, HEAD_DIM), cos, sin)
        v = v.reshape(bsz, s_len, N_DIFF_HEADS, V_DIM)
        lam = (jnp.exp(jnp.sum(lambda_q1[l].astype(jnp.float32) * lambda_k1[l].astype(jnp.float32)))
               - jnp.exp(jnp.sum(lambda_q2[l].astype(jnp.float32) * lambda_k2[l].astype(jnp.float32)))
               + lambda_init)
        attn = diff_attention(q, k, v, lam)
        attn = (rms_norm(attn, subln_g[l]) * (1.0 - lambda_init)).reshape(bsz, s_len, ATTN_WIDTH)
        ssm = s5_mixer(u, ssm_lambda_re[l], ssm_lambda_im[l], ssm_log_step[l],
                       ssm_b_re[l], ssm_b_im[l], ssm_c_re[l], ssm_c_im[l], ssm_d[l],
                       ssm_w_glu[l], ssm_b_glu[l])
        mixed = jnp.concatenate([attn, ssm.astype(attn.dtype)], axis=-1) @ w_out[l]
        h = h + mixed
        hn = rms_norm(h, norm2_g[l])
        h = h + (jax.nn.silu(hn @ w_gate[l]) * (hn @ w_up[l])) @ w_down[l]
    return rms_norm(h, final_g)
```

```python
import functools
import math

import jax
import jax.numpy as jnp
from jax import lax
from jax.experimental import pallas as pl
from jax.experimental.pallas import tpu as pltpu

NORM_EPS = 1e-5
ROPE_THETA = 500000.0
LAMBDA_INIT = 0.8 - 0.6 * math.exp(-0.3 * 0)
LAM_RE_MAX = -1e-4
CHUNK = 16
V7X_VMEM_LIMIT = 56 * 1024 * 1024

IN_PROJ_ROWS = 512
ATTN_BQ = 256
SCAN_BLOCK = 128
FFN_ROWS = 512
FFN_TILE = 256

_NT = (((1,), (1,)), ((), ()))


def _rms(x, g):
    return x * lax.rsqrt(jnp.mean(x * x, axis=-1, keepdims=True) + NORM_EPS) * g


def _in_proj_kernel(x_ref, pos_ref, g_ref, invf_ref, wT_ref, wu_ref,
                    qT_ref, k_ref, vT_ref, u_ref, *, n_maps, head_dim, rot_dim, qk_width):
    hn = _rms(x_ref[...], g_ref[...]).astype(jnp.bfloat16)
    pT = lax.dot_general(wT_ref[...], hn, _NT, preferred_element_type=jnp.float32)
    u_ref[...] = jnp.dot(hn, wu_ref[...], preferred_element_type=jnp.float32)

    ang = invf_ref[...] * pos_ref[0].astype(jnp.float32)
    cos, sin = jnp.cos(ang), jnp.sin(ang)
    half = rot_dim // 2
    c0, c1, s0, s1 = cos[:half], cos[half:], sin[:half], sin[half:]

    def rope(t, scale):
        rows = []
        for m in range(n_maps):
            r = m * head_dim
            x0, x1 = t[r:r + half], t[r + half:r + rot_dim]
            rows += [x0 * c0 - x1 * s0, x1 * c1 + x0 * s1, t[r + rot_dim:r + head_dim]]
        out = jnp.concatenate(rows, axis=0)
        return out * scale if scale != 1.0 else out

    qT_ref[...] = rope(pT[:qk_width], head_dim ** -0.5).astype(jnp.bfloat16)
    k_ref[...] = rope(pT[qk_width:2 * qk_width], 1.0).T.astype(jnp.bfloat16)
    vT = pT[2 * qk_width:].astype(jnp.bfloat16)
    vT_ref[...] = vT.reshape(vT_ref.shape)


def _in_proj(x, pos, g, invf, wT, wu, *, n_heads, head_dim, rot_dim, v_dim):
    s_len, d = x.shape
    bm = IN_PROJ_ROWS
    qk_width = n_heads * 2 * head_dim
    v_width = n_heads * v_dim
    kern = functools.partial(_in_proj_kernel, n_maps=2 * n_heads, head_dim=head_dim,
                             rot_dim=rot_dim, qk_width=qk_width)
    const = lambda i: (0, 0)
    return pl.pallas_call(
        kern,
        grid=(s_len // bm,),
        in_specs=[pl.BlockSpec((bm, d), lambda i: (i, 0)),
                  pl.BlockSpec((1, 1, bm), lambda i: (i, 0, 0)),
                  pl.BlockSpec((1, d), const),
                  pl.BlockSpec((rot_dim, 1), const),
                  pl.BlockSpec(wT.shape, const),
                  pl.BlockSpec(wu.shape, const)],
        out_specs=[pl.BlockSpec((qk_width, bm), lambda i: (0, i)),
                   pl.BlockSpec((bm, qk_width), lambda i: (i, 0)),
                   pl.BlockSpec((n_heads, 1, v_dim, bm), lambda i: (0, i, 0, 0)),
                   pl.BlockSpec((bm, wu.shape[1]), lambda i: (i, 0))],
        out_shape=[jax.ShapeDtypeStruct((qk_width, s_len), jnp.bfloat16),
                   jax.ShapeDtypeStruct((s_len, qk_width), jnp.bfloat16),
                   jax.ShapeDtypeStruct((n_heads, s_len // bm, v_dim, bm), jnp.bfloat16),
                   jax.ShapeDtypeStruct((s_len, wu.shape[1]), jnp.float32)],
        compiler_params=pltpu.CompilerParams(dimension_semantics=("arbitrary",),
                                             vmem_limit_bytes=V7X_VMEM_LIMIT),
        name="in_proj",
    )(x, pos.reshape(s_len // bm, 1, bm), g, invf, wT, wu)


def _attn_kernel(lq1_ref, lk1_ref, lq2_ref, lk2_ref, g_ref, qT_ref, k_ref, vT_ref, o_ref,
                 m_ref, l_ref, acc_ref, *, bq, bk, head_dim):
    i = pl.program_id(1)
    qT = qT_ref[...]
    zero = jnp.zeros((head_dim, bq), qT.dtype)
    qbd = jnp.concatenate([jnp.concatenate([qT[:head_dim], zero], axis=1),
                           jnp.concatenate([zero, qT[head_dim:]], axis=1)], axis=0)

    m_ref[...] = jnp.full(m_ref.shape, -jnp.inf, jnp.float32)
    l_ref[...] = jnp.zeros(l_ref.shape, jnp.float32)
    acc_ref[...] = jnp.zeros(acc_ref.shape, jnp.float32)

    def step(j, masked):
        kb = k_ref[pl.ds(pl.multiple_of(j * bk, bk), bk), :]
        s = jnp.dot(kb, qbd, preferred_element_type=jnp.float32)
        if masked:
            kpos = j * bk + lax.broadcasted_iota(jnp.int32, s.shape, 0)
            qcol = lax.broadcasted_iota(jnp.int32, s.shape, 1)
            qpos = i * bq + jnp.where(qcol >= bq, qcol - bq, qcol)
            s = jnp.where(kpos <= qpos, s, -jnp.inf)
        m_old = m_ref[...]
        m_new = jnp.maximum(m_old, jnp.max(s, axis=0, keepdims=True))
        alpha = jnp.exp(m_old - m_new)
        p = jnp.exp(s - m_new)
        l_ref[...] = alpha * l_ref[...] + jnp.sum(p, axis=0, keepdims=True)
        pv = jnp.dot(vT_ref[0, j], p.astype(jnp.bfloat16), preferred_element_type=jnp.float32)
        acc_ref[...] = alpha * acc_ref[...] + pv
        m_ref[...] = m_new

    n_full = (i * bq) // bk

    def body(j, carry):
        step(j, masked=False)
        return carry

    lax.fori_loop(0, n_full, body, 0)
    step(n_full, masked=True)

    lam = (jnp.exp(jnp.sum(lq1_ref[...] * lk1_ref[...], axis=-1, keepdims=True))
           - jnp.exp(jnp.sum(lq2_ref[...] * lk2_ref[...], axis=-1, keepdims=True))
           + LAMBDA_INIT)
    o = acc_ref[...] / l_ref[...]
    oT = o[:, :bq] - lam * o[:, bq:]
    y = oT * lax.rsqrt(jnp.mean(oT * oT, axis=0, keepdims=True) + NORM_EPS) * g_ref[...]
    o_ref[...] = (y * (1.0 - LAMBDA_INIT)).T.astype(o_ref.dtype)


def _attention(lq1, lk1, lq2, lk2, subln_g, qT, k, vT, *, head_dim):
    n_heads, n_kblk, v_dim, bk = vT.shape
    s_len = k.shape[0]
    bq = ATTN_BQ
    kern = functools.partial(_attn_kernel, bq=bq, bk=bk, head_dim=head_dim)
    vec = pl.BlockSpec((1, head_dim), lambda h, i: (0, 0))
    return pl.pallas_call(
        kern,
        grid=(n_heads, s_len // bq),
        in_specs=[vec, vec, vec, vec,
                  pl.BlockSpec((v_dim, 1), lambda h, i: (0, 0)),
                  pl.BlockSpec((2 * head_dim, bq), lambda h, i: (h, i)),
                  pl.BlockSpec((s_len, 2 * head_dim), lambda h, i: (0, h)),
                  pl.BlockSpec((1, n_kblk, v_dim, bk), lambda h, i: (h, 0, 0, 0))],
        out_specs=pl.BlockSpec((bq, v_dim), lambda h, i: (i, h)),
        out_shape=jax.ShapeDtypeStruct((s_len, n_heads * v_dim), jnp.bfloat16),
        scratch_shapes=[pltpu.VMEM((1, 2 * bq), jnp.float32),
                        pltpu.VMEM((1, 2 * bq), jnp.float32),
                        pltpu.VMEM((v_dim, 2 * bq), jnp.float32)],
        compiler_params=pltpu.CompilerParams(dimension_semantics=("arbitrary", "arbitrary"),
                                             vmem_limit_bytes=V7X_VMEM_LIMIT),
        name="attn",
    )(lq1, lk1, lq2, lk2, subln_g.reshape(v_dim, 1), qT, k, vT)


def _s5_matrices(lam_re, lam_im, log_step, b_re, b_im, c_re, c_im):
    f32 = jnp.float32
    hi = lax.Precision.HIGHEST
    n_groups, n_state, n_ch = b_re.shape
    lr = jnp.minimum(lam_re.astype(f32), LAM_RE_MAX)
    li = lam_im.astype(f32)
    step = jnp.exp(log_step.astype(f32))[:, None]
    mag = jnp.exp(lr * step)
    lb_re, lb_im = mag * jnp.cos(li * step), mag * jnp.sin(li * step)
    denom = lr * lr + li * li
    nr, ni = lb_re - 1.0, lb_im
    coef_re = (nr * lr + ni * li) / denom
    coef_im = (ni * lr - nr * li) / denom
    br, bi = b_re.astype(f32), b_im.astype(f32)
    bb_re = coef_re[..., None] * br - coef_im[..., None] * bi
    bb_im = coef_re[..., None] * bi + coef_im[..., None] * br
    tau = jnp.arange(CHUNK + 1, dtype=f32)[None, :, None]
    pmag = jnp.exp(lr[:, None, :] * step[:, None, :] * tau)
    pw_re = pmag * jnp.cos(li[:, None, :] * step[:, None, :] * tau)
    pw_im = pmag * jnp.sin(li[:, None, :] * step[:, None, :] * tau)
    cr, ci = c_re.astype(f32)[:, None], c_im.astype(f32)[:, None]
    cp_re = cr * pw_re[:, :, None, :] - ci * pw_im[:, :, None, :]
    cp_im = cr * pw_im[:, :, None, :] + ci * pw_re[:, :, None, :]
    kmat = (jnp.einsum('gthp,gpk->gthk', cp_re, bb_re, precision=hi)
            - jnp.einsum('gthp,gpk->gthk', cp_im, bb_im, precision=hi))
    j = jnp.arange(CHUNK)[:, None]
    t = jnp.arange(CHUNK)[None, :]
    lag = t - j
    ksel = jnp.where((lag >= 0)[None, :, :, None, None], kmat[:, jnp.maximum(lag, 0)], 0.0)
    tmat = ksel.transpose(0, 1, 4, 2, 3).reshape(n_groups, CHUNK * n_ch, CHUNK * n_ch)
    rev = CHUNK - 1 - jnp.arange(CHUNK)
    ar, ai = pw_re[:, rev][:, :, None, :], pw_im[:, rev][:, :, None, :]
    brt, bit = bb_re.transpose(0, 2, 1)[:, None], bb_im.transpose(0, 2, 1)[:, None]
    w_re = (ar * brt - ai * bit).reshape(n_groups, CHUNK * n_ch, n_state)
    w_im = (ar * bit + ai * brt).reshape(n_groups, CHUNK * n_ch, n_state)
    wmat = jnp.concatenate([w_re, w_im, w_im, w_re], axis=-1)
    e_re = cp_re[:, 1:].transpose(0, 3, 1, 2).reshape(n_groups, n_state, CHUNK * n_ch)
    e_im = cp_im[:, 1:].transpose(0, 3, 1, 2).reshape(n_groups, n_state, CHUNK * n_ch)
    emat = jnp.concatenate([e_re, -e_im], axis=1)
    al_re, al_im = pw_re[:, CHUNK], pw_im[:, CHUNK]
    a1 = jnp.concatenate([al_re, al_re], axis=-1)
    a2 = jnp.concatenate([-al_im, al_im], axis=-1)
    bf = jnp.bfloat16
    return tmat.astype(bf), wmat.astype(bf), emat.astype(bf), a1, a2, -a2


def _s5_state_kernel(u_ref, w_ref, s_ref):
    s_ref[0] = jnp.dot(u_ref[0], w_ref[0], preferred_element_type=jnp.float32)


def _s5_scan_kernel(s_ref, a1_ref, a2_ref, a2s_ref, x0_ref, x_ref, xs_ref, *, n_state):
    @pl.when(pl.program_id(0) == 0)
    def _():
        x_ref[...] = jnp.zeros(x_ref.shape, jnp.float32)
        xs_ref[...] = jnp.zeros(xs_ref.shape, jnp.float32)

    a1, a2, a2s = a1_ref[...], a2_ref[...], a2s_ref[...]
    w = 2 * n_state

    def body(c, carry):
        x, xs = carry
        x0_ref[c] = x
        s = s_ref[c]
        return a1 * x + a2 * xs + s[:, :w], a1 * xs + a2s * x + s[:, w:]

    x, xs = lax.fori_loop(0, s_ref.shape[0], body, (x_ref[...], xs_ref[...]), unroll=4)
    x_ref[...] = x
    xs_ref[...] = xs


def _s5_out_kernel(u_ref, x0_ref, t_ref, e_ref, y_ref):
    y_ref[0] = (jnp.dot(u_ref[0], t_ref[0], preferred_element_type=jnp.float32)
                + jnp.dot(x0_ref[0], e_ref[0], preferred_element_type=jnp.float32))


def _s5_sequence(u, tmat, wmat, emat, a1, a2, a2s):
    s_len = u.shape[0]
    n_groups, cw, _ = tmat.shape
    n_ch = cw // CHUNK
    n_state = a1.shape[1] // 2
    n_chunks = s_len // CHUNK
    params = pltpu.CompilerParams(dimension_semantics=("arbitrary",), vmem_limit_bytes=V7X_VMEM_LIMIT)
    grp = lambda g: (g, 0, 0)
    u_flat = (u.astype(jnp.bfloat16).reshape(n_chunks, CHUNK, n_groups, n_ch)
              .transpose(2, 0, 1, 3).reshape(n_groups, n_chunks, cw))
    s_loc = pl.pallas_call(
        _s5_state_kernel,
        grid=(n_groups,),
        in_specs=[pl.BlockSpec((1, n_chunks, cw), grp), pl.BlockSpec((1, cw, 4 * n_state), grp)],
        out_specs=pl.BlockSpec((1, n_chunks, 4 * n_state), grp),
        out_shape=jax.ShapeDtypeStruct((n_groups, n_chunks, 4 * n_state), jnp.float32),
        compiler_params=params, name="s5_state",
    )(u_flat, wmat)
    cb = SCAN_BLOCK
    x0 = pl.pallas_call(
        functools.partial(_s5_scan_kernel, n_state=n_state),
        grid=(n_chunks // cb,),
        in_specs=[pl.BlockSpec((cb, n_groups, 4 * n_state), lambda i: (i, 0, 0)),
                  pl.BlockSpec(a1.shape, lambda i: (0, 0)),
                  pl.BlockSpec(a2.shape, lambda i: (0, 0)),
                  pl.BlockSpec(a2s.shape, lambda i: (0, 0))],
        out_specs=pl.BlockSpec((cb, n_groups, 2 * n_state), lambda i: (i, 0, 0)),
        out_shape=jax.ShapeDtypeStruct((n_chunks, n_groups, 2 * n_state), jnp.float32),
        scratch_shapes=[pltpu.VMEM(a1.shape, jnp.float32), pltpu.VMEM(a1.shape, jnp.float32)],
        compiler_params=params, name="s5_scan",
    )(s_loc.transpose(1, 0, 2), a1, a2, a2s)
    y_flat = pl.pallas_call(
        _s5_out_kernel,
        grid=(n_groups,),
        in_specs=[pl.BlockSpec((1, n_chunks, cw), grp),
                  pl.BlockSpec((1, n_chunks, 2 * n_state), grp),
                  pl.BlockSpec((1, cw, cw), grp),
                  pl.BlockSpec((1, 2 * n_state, cw), grp)],
        out_specs=pl.BlockSpec((1, n_chunks, cw), grp),
        out_shape=jax.ShapeDtypeStruct((n_groups, n_chunks, cw), jnp.float32),
        compiler_params=params, name="s5_out",
    )(u_flat, x0.transpose(1, 0, 2).astype(jnp.bfloat16), tmat, emat)
    return (y_flat.reshape(n_groups, n_chunks, CHUNK, n_ch)
            .transpose(1, 2, 0, 3).reshape(s_len, n_groups * n_ch))


def _mix_ffn_kernel(x_ref, attn_ref, y_ref, u_ref, d_ref, wglu_ref, bglu_ref, woa_ref, wos_ref,
                    g2_ref, wg_ref, wu_ref, wd_ref, gf_ref, o_ref, h_ref, hn_ref, acc_ref):
    f = pl.program_id(1)

    @pl.when(f == 0)
    def _():
        y = jax.nn.gelu(y_ref[...] + d_ref[...] * u_ref[...])
        gate = jnp.dot(y.astype(jnp.bfloat16), wglu_ref[...],
                       preferred_element_type=jnp.float32) + bglu_ref[...]
        ssm = y * jax.nn.sigmoid(gate)
        mixed = (jnp.dot(attn_ref[...], woa_ref[...], preferred_element_type=jnp.float32)
                 + jnp.dot(ssm.astype(jnp.bfloat16), wos_ref[...], preferred_element_type=jnp.float32))
        h = x_ref[...] + mixed
        h_ref[...] = h
        hn_ref[...] = _rms(h, g2_ref[...]).astype(jnp.bfloat16)
        acc_ref[...] = jnp.zeros(acc_ref.shape, jnp.float32)

    hn = hn_ref[...]
    gate = jnp.dot(hn, wg_ref[...], preferred_element_type=jnp.float32)
    up = jnp.dot(hn, wu_ref[...], preferred_element_type=jnp.float32)
    act = (jax.nn.silu(gate) * up).astype(jnp.bfloat16)
    acc_ref[...] += jnp.dot(act, wd_ref[...], preferred_element_type=jnp.float32)

    @pl.when(f == pl.num_programs(1) - 1)
    def _():
        o_ref[...] = _rms(h_ref[...] + acc_ref[...], gf_ref[...])


def _mix_ffn(x, attn, y, u, d, wglu, bglu, wo_attn, wo_ssm, g2, wg, wu, wd, gf):
    s_len, dm = x.shape
    bm, tf = FFN_ROWS, FFN_TILE
    d_ff = wg.shape[1]
    row = lambda w: pl.BlockSpec((bm, w), lambda i, f: (i, 0))
    const = lambda a: pl.BlockSpec(a.shape, lambda i, f: (0, 0))
    return pl.pallas_call(
        _mix_ffn_kernel,
        grid=(s_len // bm, d_ff // tf),
        in_specs=[row(dm), row(attn.shape[1]), row(y.shape[1]), row(u.shape[1]),
                  const(d), const(wglu), const(bglu), const(wo_attn), const(wo_ssm), const(g2),
                  pl.BlockSpec((dm, tf), lambda i, f: (0, f)),
                  pl.BlockSpec((dm, tf), lambda i, f: (0, f)),
                  pl.BlockSpec((tf, dm), lambda i, f: (f, 0)),
                  const(gf)],
        out_specs=row(dm),
        out_shape=jax.ShapeDtypeStruct((s_len, dm), jnp.float32),
        scratch_shapes=[pltpu.VMEM((bm, dm), jnp.float32),
                        pltpu.VMEM((bm, dm), jnp.bfloat16),
                        pltpu.VMEM((bm, dm), jnp.float32)],
        compiler_params=pltpu.CompilerParams(dimension_semantics=("arbitrary", "arbitrary"),
                                             vmem_limit_bytes=V7X_VMEM_LIMIT),
        name="mix_ffn",
    )(x, attn, y, u, d, wglu, bglu, wo_attn, wo_ssm, g2, wg, wu, wd, gf)


def kernel(x, positions, norm1_g, w_in, lambda_q1, lambda_k1, lambda_q2, lambda_k2, subln_g, ssm_lambda_re, ssm_lambda_im, ssm_log_step, ssm_b_re, ssm_b_im, ssm_c_re, ssm_c_im, ssm_d, ssm_w_glu, ssm_b_glu, w_out, norm2_g, w_gate, w_up, w_down, final_g):
    bsz, s_len, d_model = x.shape
    assert bsz == 1 and norm1_g.shape[0] == 1, "one sequence, one layer"
    assert s_len % max(IN_PROJ_ROWS, FFN_ROWS, CHUNK * SCAN_BLOCK) == 0
    f32, bf16 = jnp.float32, jnp.bfloat16
    head_dim = lambda_q1.shape[-1]
    v_dim = subln_g.shape[-1]
    rot_dim = head_dim // 4
    n_groups, n_state, n_ch = ssm_b_re.shape[1:]
    ssm_width = n_groups * n_ch
    attn_width = d_model - ssm_width
    n_heads = attn_width // v_dim
    qk_width = n_heads * 2 * head_dim
    assert w_in.shape[-1] == 2 * qk_width + attn_width + ssm_width
    assert w_gate.shape[-1] % FFN_TILE == 0

    inv_freq = ROPE_THETA ** (-jnp.arange(0, rot_dim, 2, dtype=f32) / rot_dim)
    invf = jnp.concatenate([inv_freq, inv_freq]).reshape(rot_dim, 1)
    w = w_in[0]
    n_qkv = 2 * qk_width + attn_width
    qT, k, vT, u = _in_proj(x[0], positions[0], norm1_g[0].reshape(1, d_model).astype(f32), invf,
                            w[:, :n_qkv].T.astype(bf16), w[:, n_qkv:].astype(bf16),
                            n_heads=n_heads, head_dim=head_dim, rot_dim=rot_dim, v_dim=v_dim)

    lam_vec = lambda a: a[0].reshape(1, head_dim).astype(f32)
    attn = _attention(lam_vec(lambda_q1), lam_vec(lambda_k1), lam_vec(lambda_q2), lam_vec(lambda_k2),
                      subln_g[0].astype(f32), qT, k, vT, head_dim=head_dim)

    mats = _s5_matrices(ssm_lambda_re[0], ssm_lambda_im[0], ssm_log_step[0], ssm_b_re[0], ssm_b_im[0],
                        ssm_c_re[0], ssm_c_im[0])
    y = _s5_sequence(u, *mats)

    wo = w_out[0].astype(bf16)
    out = _mix_ffn(x[0], attn, y, u,
                   ssm_d[0].reshape(1, ssm_width).astype(f32), ssm_w_glu[0].astype(bf16),
                   ssm_b_glu[0].reshape(1, ssm_width).astype(f32), wo[:attn_width], wo[attn_width:],
                   norm2_g[0].reshape(1, d_model).astype(f32),
                   w_gate[0].astype(bf16), w_up[0].astype(bf16), w_down[0].astype(bf16),
                   final_g.reshape(1, d_model).astype(f32))
    return out[None]
```

```python
import functools
import math

import jax
import jax.numpy as jnp
from jax import lax
from jax.experimental import pallas as pl
from jax.experimental.pallas import tpu as pltpu

NORM_EPS = 1e-5
ROPE_THETA = 500000.0
LAMBDA_INIT = 0.8 - 0.6 * math.exp(-0.3 * 0)
LAM_RE_MAX = -1e-4
LOG2_E = math.log2(math.e)
V_EXT_ROWS = 16
CHUNK = 16
V7X_VMEM_LIMIT = 56 * 1024 * 1024

IN_PROJ_ROWS = 512
ATTN_BQ = 256
ATTN_HEADS_PER_STEP = 2
SCAN_BLOCK = 128
FFN_ROWS = 512
FFN_TILE = 256

_NT = (((1,), (1,)), ((), ()))


def _rms(x, g):
    return x * lax.rsqrt(jnp.mean(x * x, axis=-1, keepdims=True) + NORM_EPS) * g


def _in_proj_kernel(x_ref, pos_ref, g_ref, invf_ref, wT_ref, wu_ref,
                    qT_ref, k_ref, vT_ref, u_ref, *, n_maps, head_dim, rot_dim, qk_width):
    hn = _rms(x_ref[...], g_ref[...]).astype(jnp.bfloat16)
    pT = lax.dot_general(wT_ref[...], hn, _NT, preferred_element_type=jnp.float32)
    u_ref[...] = jnp.dot(hn, wu_ref[...], preferred_element_type=jnp.float32)

    ang = invf_ref[...] * pos_ref[0].astype(jnp.float32)
    cos, sin = jnp.cos(ang), jnp.sin(ang)
    half = rot_dim // 2
    c0, c1, s0, s1 = cos[:half], cos[half:], sin[:half], sin[half:]

    def rope(t, scale):
        rows = []
        for m in range(n_maps):
            r = m * head_dim
            x0, x1 = t[r:r + half], t[r + half:r + rot_dim]
            rows += [x0 * c0 - x1 * s0, x1 * c1 + x0 * s1, t[r + rot_dim:r + head_dim]]
        out = jnp.concatenate(rows, axis=0)
        return out * scale if scale != 1.0 else out

    qT_ref[...] = rope(pT[:qk_width], head_dim ** -0.5 * LOG2_E).astype(jnp.bfloat16)
    k_ref[...] = rope(pT[qk_width:2 * qk_width], 1.0).T.astype(jnp.bfloat16)
    n_heads, _, v_rows, bm = vT_ref.shape
    vT = pT[2 * qk_width:].astype(jnp.bfloat16).reshape(n_heads, v_rows - V_EXT_ROWS, bm)
    ext_row = lax.broadcasted_iota(jnp.int32, (n_heads, V_EXT_ROWS, bm), 1)
    ext = jnp.where(ext_row == 0, 1.0, 0.0).astype(jnp.bfloat16)
    vT_ref[...] = jnp.concatenate([vT, ext], axis=1).reshape(vT_ref.shape)


def _in_proj(x, pos, g, invf, wT, wu, *, n_heads, head_dim, rot_dim, v_dim):
    s_len, d = x.shape
    bm = IN_PROJ_ROWS
    qk_width = n_heads * 2 * head_dim
    v_width = n_heads * v_dim
    kern = functools.partial(_in_proj_kernel, n_maps=2 * n_heads, head_dim=head_dim,
                             rot_dim=rot_dim, qk_width=qk_width)
    const = lambda i: (0, 0)
    return pl.pallas_call(
        kern,
        grid=(s_len // bm,),
        in_specs=[pl.BlockSpec((bm, d), lambda i: (i, 0)),
                  pl.BlockSpec((1, 1, bm), lambda i: (i, 0, 0)),
                  pl.BlockSpec((1, d), const),
                  pl.BlockSpec((rot_dim, 1), const),
                  pl.BlockSpec(wT.shape, const),
                  pl.BlockSpec(wu.shape, const)],
        out_specs=[pl.BlockSpec((qk_width, bm), lambda i: (0, i)),
                   pl.BlockSpec((bm, qk_width), lambda i: (i, 0)),
                   pl.BlockSpec((n_heads, 1, v_dim + V_EXT_ROWS, bm), lambda i: (0, i, 0, 0)),
                   pl.BlockSpec((bm, wu.shape[1]), lambda i: (i, 0))],
        out_shape=[jax.ShapeDtypeStruct((qk_width, s_len), jnp.bfloat16),
                   jax.ShapeDtypeStruct((s_len, qk_width), jnp.bfloat16),
                   jax.ShapeDtypeStruct((n_heads, s_len // bm, v_dim + V_EXT_ROWS, bm), jnp.bfloat16),
                   jax.ShapeDtypeStruct((s_len, wu.shape[1]), jnp.float32)],
        compiler_params=pltpu.CompilerParams(dimension_semantics=("arbitrary",),
                                             vmem_limit_bytes=V7X_VMEM_LIMIT),
        name="in_proj",
    )(x, pos.reshape(s_len // bm, 1, bm), g, invf, wT, wu)


def _attn_kernel(lq1_ref, lk1_ref, lq2_ref, lk2_ref, g_ref, qT_ref, k_ref, vT_ref, o_ref,
                 qbd_ref, s_ref, mb_ref, p_ref, al_ref, m_ref, acc_ref,
                 *, bq, bk, head_dim, v_dim, n_par):
    i = pl.program_id(1)
    kw = 2 * head_dim
    n_full = (i * bq) // bk

    zero = jnp.zeros((head_dim, bq), qT_ref.dtype)
    for h in range(n_par):
        qT = qT_ref[h * kw:(h + 1) * kw, :]
        qbd_ref[h] = jnp.concatenate([jnp.concatenate([qT[:head_dim], zero], axis=1),
                                      jnp.concatenate([zero, qT[head_dim:]], axis=1)], axis=0)
    m_ref[...] = jnp.full(m_ref.shape, -jnp.inf, jnp.float32)
    acc_ref[...] = jnp.zeros(acc_ref.shape, jnp.float32)
    p_ref[...] = jnp.zeros(p_ref.shape, p_ref.dtype)
    al_ref[...] = jnp.ones(al_ref.shape, jnp.float32)

    def scores(t, h):
        kb = k_ref[pl.ds(pl.multiple_of(t * bk, bk), bk), h * kw:(h + 1) * kw]
        return jnp.dot(kb, qbd_ref[h], preferred_element_type=jnp.float32)

    def tick(t, *, s_rd, s_wr, p_rd, p_wr, qk, masked):
        for h in range(n_par):
            if qk:
                s_next = scores(t + 1, h)
                s_ref[s_wr, h] = s_next
                mb_ref[s_wr, h] = jnp.max(s_next, axis=0, keepdims=True)
            s = s_ref[s_rd, h]
            if masked:
                kpos = t * bk + lax.broadcasted_iota(jnp.int32, s.shape, 0)
                qcol = lax.broadcasted_iota(jnp.int32, s.shape, 1)
                qpos = i * bq + jnp.where(qcol >= bq, qcol - bq, qcol)
                s = jnp.where(kpos <= qpos, s, -jnp.inf)
                m_blk = jnp.max(s, axis=0, keepdims=True)
            else:
                m_blk = mb_ref[s_rd, h]
            m_old = m_ref[h]
            m_new = jnp.maximum(m_old, m_blk)
            p_ref[p_wr, h] = jnp.exp2(s - m_new).astype(p_ref.dtype)
            al_ref[p_wr, h] = jnp.exp2(m_old - m_new)
            m_ref[h] = m_new
            pv = jnp.dot(vT_ref[h, jnp.maximum(t - 1, 0)], p_ref[p_rd, h],
                         preferred_element_type=jnp.float32)
            acc_ref[h] = al_ref[p_rd, h] * acc_ref[h] + pv

    even = dict(s_rd=0, s_wr=1, p_rd=1, p_wr=0)
    odd = dict(s_rd=1, s_wr=0, p_rd=0, p_wr=1)

    for h in range(n_par):
        s0 = scores(0, h)
        m0 = jnp.max(s0, axis=0, keepdims=True)
        for slot in range(2):
            s_ref[slot, h] = s0
            mb_ref[slot, h] = m0

    lead = n_full % 2

    @pl.when(lead == 1)
    def _():
        tick(0, qk=True, masked=False, **odd)

    def pair(c, carry):
        t = lead + 2 * c
        tick(t, qk=True, masked=False, **even)
        tick(t + 1, qk=True, masked=False, **odd)
        return carry

    lax.fori_loop(0, n_full // 2, pair, 0)
    tick(n_full, qk=False, masked=True, **even)

    lam = (jnp.exp(jnp.sum(lq1_ref[...] * lk1_ref[...], axis=-1, keepdims=True))
           - jnp.exp(jnp.sum(lq2_ref[...] * lk2_ref[...], axis=-1, keepdims=True))
           + LAMBDA_INIT)
    for h in range(n_par):
        pv = jnp.dot(vT_ref[h, n_full], p_ref[even["p_wr"], h], preferred_element_type=jnp.float32)
        acc = al_ref[even["p_wr"], h] * acc_ref[h] + pv
        o = acc[:v_dim] / acc[v_dim:v_dim + 1]
        oT = o[:, :bq] - lam * o[:, bq:]
        y = oT * lax.rsqrt(jnp.mean(oT * oT, axis=0, keepdims=True) + NORM_EPS) * g_ref[...]
        o_ref[:, h * v_dim:(h + 1) * v_dim] = (y * (1.0 - LAMBDA_INIT)).T.astype(o_ref.dtype)


def _attention(lq1, lk1, lq2, lk2, subln_g, qT, k, vT, *, head_dim):
    n_heads, n_kblk, v_rows, bk = vT.shape
    v_dim = v_rows - V_EXT_ROWS
    s_len = k.shape[0]
    bq, n_par = ATTN_BQ, ATTN_HEADS_PER_STEP
    kern = functools.partial(_attn_kernel, bq=bq, bk=bk, head_dim=head_dim, v_dim=v_dim, n_par=n_par)
    vec = pl.BlockSpec((1, head_dim), lambda h, i: (0, 0))
    return pl.pallas_call(
        kern,
        grid=(n_heads // n_par, s_len // bq),
        in_specs=[vec, vec, vec, vec,
                  pl.BlockSpec((v_dim, 1), lambda h, i: (0, 0)),
                  pl.BlockSpec((n_par * 2 * head_dim, bq), lambda h, i: (h, i)),
                  pl.BlockSpec((s_len, n_par * 2 * head_dim), lambda h, i: (0, h)),
                  pl.BlockSpec((n_par, n_kblk, v_rows, bk), lambda h, i: (h, 0, 0, 0))],
        out_specs=pl.BlockSpec((bq, n_par * v_dim), lambda h, i: (i, h)),
        out_shape=jax.ShapeDtypeStruct((s_len, n_heads * v_dim), jnp.bfloat16),
        scratch_shapes=[pltpu.VMEM((n_par, 2 * head_dim, 2 * bq), jnp.bfloat16),
                        pltpu.VMEM((2, n_par, bk, 2 * bq), jnp.float32),
                        pltpu.VMEM((2, n_par, 1, 2 * bq), jnp.float32),
                        pltpu.VMEM((2, n_par, bk, 2 * bq), jnp.bfloat16),
                        pltpu.VMEM((2, n_par, 1, 2 * bq), jnp.float32),
                        pltpu.VMEM((n_par, 1, 2 * bq), jnp.float32),
                        pltpu.VMEM((n_par, v_rows, 2 * bq), jnp.float32)],
        compiler_params=pltpu.CompilerParams(dimension_semantics=("arbitrary", "arbitrary"),
                                             vmem_limit_bytes=V7X_VMEM_LIMIT),
        name="attn",
    )(lq1, lk1, lq2, lk2, subln_g.reshape(v_dim, 1), qT, k, vT)


def _s5_matrices(lam_re, lam_im, log_step, b_re, b_im, c_re, c_im):
    f32 = jnp.float32
    hi = lax.Precision.HIGHEST
    n_groups, n_state, n_ch = b_re.shape
    lr = jnp.minimum(lam_re.astype(f32), LAM_RE_MAX)
    li = lam_im.astype(f32)
    step = jnp.exp(log_step.astype(f32))[:, None]
    mag = jnp.exp(lr * step)
    lb_re, lb_im = mag * jnp.cos(li * step), mag * jnp.sin(li * step)
    denom = lr * lr + li * li
    nr, ni = lb_re - 1.0, lb_im
    coef_re = (nr * lr + ni * li) / denom
    coef_im = (ni * lr - nr * li) / denom
    br, bi = b_re.astype(f32), b_im.astype(f32)
    bb_re = coef_re[..., None] * br - coef_im[..., None] * bi
    bb_im = coef_re[..., None] * bi + coef_im[..., None] * br
    tau = jnp.arange(CHUNK + 1, dtype=f32)[None, :, None]
    pmag = jnp.exp(lr[:, None, :] * step[:, None, :] * tau)
    pw_re = pmag * jnp.cos(li[:, None, :] * step[:, None, :] * tau)
    pw_im = pmag * jnp.sin(li[:, None, :] * step[:, None, :] * tau)
    cr, ci = c_re.astype(f32)[:, None], c_im.astype(f32)[:, None]
    cp_re = cr * pw_re[:, :, None, :] - ci * pw_im[:, :, None, :]
    cp_im = cr * pw_im[:, :, None, :] + ci * pw_re[:, :, None, :]
    kmat = (jnp.einsum('gthp,gpk->gthk', cp_re, bb_re, precision=hi)
            - jnp.einsum('gthp,gpk->gthk', cp_im, bb_im, precision=hi))
    j = jnp.arange(CHUNK)[:, None]
    t = jnp.arange(CHUNK)[None, :]
    lag = t - j
    ksel = jnp.where((lag >= 0)[None, :, :, None, None], kmat[:, jnp.maximum(lag, 0)], 0.0)
    tmat = ksel.transpose(0, 1, 4, 2, 3).reshape(n_groups, CHUNK * n_ch, CHUNK * n_ch)
    rev = CHUNK - 1 - jnp.arange(CHUNK)
    ar, ai = pw_re[:, rev][:, :, None, :], pw_im[:, rev][:, :, None, :]
    brt, bit = bb_re.transpose(0, 2, 1)[:, None], bb_im.transpose(0, 2, 1)[:, None]
    w_re = (ar * brt - ai * bit).reshape(n_groups, CHUNK * n_ch, n_state)
    w_im = (ar * bit + ai * brt).reshape(n_groups, CHUNK * n_ch, n_state)
    wmat = jnp.concatenate([w_re, w_im, w_im, w_re], axis=-1)
    e_re = cp_re[:, 1:].transpose(0, 3, 1, 2).reshape(n_groups, n_state, CHUNK * n_ch)
    e_im = cp_im[:, 1:].transpose(0, 3, 1, 2).reshape(n_groups, n_state, CHUNK * n_ch)
    emat = jnp.concatenate([e_re, -e_im], axis=1)
    al_re, al_im = pw_re[:, CHUNK], pw_im[:, CHUNK]
    a1 = jnp.concatenate([al_re, al_re], axis=-1)
    a2 = jnp.concatenate([-al_im, al_im], axis=-1)
    bf = jnp.bfloat16
    return tmat.astype(bf), wmat.astype(bf), emat.astype(bf), a1, a2, -a2


def _s5_state_kernel(u_ref, w_ref, s_ref):
    s_ref[0] = jnp.dot(u_ref[0], w_ref[0], preferred_element_type=jnp.float32)


def _s5_scan_kernel(s_ref, a1_ref, a2_ref, a2s_ref, x0_ref, x_ref, xs_ref, *, n_state):
    @pl.when(pl.program_id(0) == 0)
    def _():
        x_ref[...] = jnp.zeros(x_ref.shape, jnp.float32)
        xs_ref[...] = jnp.zeros(xs_ref.shape, jnp.float32)

    a1, a2, a2s = a1_ref[...], a2_ref[...], a2s_ref[...]
    w = 2 * n_state

    def body(c, carry):
        x, xs = carry
        x0_ref[c] = x
        s = s_ref[c]
        return a1 * x + a2 * xs + s[:, :w], a1 * xs + a2s * x + s[:, w:]

    x, xs = lax.fori_loop(0, s_ref.shape[0], body, (x_ref[...], xs_ref[...]), unroll=4)
    x_ref[...] = x
    xs_ref[...] = xs


def _s5_out_kernel(u_ref, x0_ref, t_ref, e_ref, y_ref):
    y_ref[0] = (jnp.dot(u_ref[0], t_ref[0], preferred_element_type=jnp.float32)
                + jnp.dot(x0_ref[0], e_ref[0], preferred_element_type=jnp.float32))


def _s5_sequence(u, tmat, wmat, emat, a1, a2, a2s):
    s_len = u.shape[0]
    n_groups, cw, _ = tmat.shape
    n_ch = cw // CHUNK
    n_state = a1.shape[1] // 2
    n_chunks = s_len // CHUNK
    params = pltpu.CompilerParams(dimension_semantics=("arbitrary",), vmem_limit_bytes=V7X_VMEM_LIMIT)
    grp = lambda g: (g, 0, 0)
    u_flat = (u.astype(jnp.bfloat16).reshape(n_chunks, CHUNK, n_groups, n_ch)
              .transpose(2, 0, 1, 3).reshape(n_groups, n_chunks, cw))
    s_loc = pl.pallas_call(
        _s5_state_kernel,
        grid=(n_groups,),
        in_specs=[pl.BlockSpec((1, n_chunks, cw), grp), pl.BlockSpec((1, cw, 4 * n_state), grp)],
        out_specs=pl.BlockSpec((1, n_chunks, 4 * n_state), grp),
        out_shape=jax.ShapeDtypeStruct((n_groups, n_chunks, 4 * n_state), jnp.float32),
        compiler_params=params, name="s5_state",
    )(u_flat, wmat)
    cb = SCAN_BLOCK
    x0 = pl.pallas_call(
        functools.partial(_s5_scan_kernel, n_state=n_state),
        grid=(n_chunks // cb,),
        in_specs=[pl.BlockSpec((cb, n_groups, 4 * n_state), lambda i: (i, 0, 0)),
                  pl.BlockSpec(a1.shape, lambda i: (0, 0)),
                  pl.BlockSpec(a2.shape, lambda i: (0, 0)),
                  pl.BlockSpec(a2s.shape, lambda i: (0, 0))],
        out_specs=pl.BlockSpec((cb, n_groups, 2 * n_state), lambda i: (i, 0, 0)),
        out_shape=jax.ShapeDtypeStruct((n_chunks, n_groups, 2 * n_state), jnp.float32),
        scratch_shapes=[pltpu.VMEM(a1.shape, jnp.float32), pltpu.VMEM(a1.shape, jnp.float32)],
        compiler_params=params, name="s5_scan",
    )(s_loc.transpose(1, 0, 2), a1, a2, a2s)
    y_flat = pl.pallas_call(
        _s5_out_kernel,
        grid=(n_groups,),
        in_specs=[pl.BlockSpec((1, n_chunks, cw), grp),
                  pl.BlockSpec((1, n_chunks, 2 * n_state), grp),
                  pl.BlockSpec((1, cw, cw), grp),
                  pl.BlockSpec((1, 2 * n_state, cw), grp)],
        out_specs=pl.BlockSpec((1, n_chunks, cw), grp),
        out_shape=jax.ShapeDtypeStruct((n_groups, n_chunks, cw), jnp.float32),
        compiler_params=params, name="s5_out",
    )(u_flat, x0.transpose(1, 0, 2).astype(jnp.bfloat16), tmat, emat)
    return (y_flat.reshape(n_groups, n_chunks, CHUNK, n_ch)
            .transpose(1, 2, 0, 3).reshape(s_len, n_groups * n_ch))


def _mix_ffn_kernel(x_ref, attn_ref, y_ref, u_ref, d_ref, wglu_ref, bglu_ref, woa_ref, wos_ref,
                    g2_ref, wg_ref, wu_ref, wd_ref, gf_ref, o_ref, h_ref, hn_ref, acc_ref):
    f = pl.program_id(1)

    @pl.when(f == 0)
    def _():
        y = jax.nn.gelu(y_ref[...] + d_ref[...] * u_ref[...])
        gate = jnp.dot(y.astype(jnp.bfloat16), wglu_ref[...],
                       preferred_element_type=jnp.float32) + bglu_ref[...]
        ssm = y * jax.nn.sigmoid(gate)
        mixed = (jnp.dot(attn_ref[...], woa_ref[...], preferred_element_type=jnp.float32)
                 + jnp.dot(ssm.astype(jnp.bfloat16), wos_ref[...], preferred_element_type=jnp.float32))
        h = x_ref[...] + mixed
        h_ref[...] = h
        hn_ref[...] = _rms(h, g2_ref[...]).astype(jnp.bfloat16)
        acc_ref[...] = jnp.zeros(acc_ref.shape, jnp.float32)

    hn = hn_ref[...]
    gate = jnp.dot(hn, wg_ref[...], preferred_element_type=jnp.float32)
    up = jnp.dot(hn, wu_ref[...], preferred_element_type=jnp.float32)
    act = (jax.nn.silu(gate) * up).astype(jnp.bfloat16)
    acc_ref[...] += jnp.dot(act, wd_ref[...], preferred_element_type=jnp.float32)

    @pl.when(f == pl.num_programs(1) - 1)
    def _():
        o_ref[...] = _rms(h_ref[...] + acc_ref[...], gf_ref[...])


def _mix_ffn(x, attn, y, u, d, wglu, bglu, wo_attn, wo_ssm, g2, wg, wu, wd, gf):
    s_len, dm = x.shape
    bm, tf = FFN_ROWS, FFN_TILE
    d_ff = wg.shape[1]
    row = lambda w: pl.BlockSpec((bm, w), lambda i, f: (i, 0))
    const = lambda a: pl.BlockSpec(a.shape, lambda i, f: (0, 0))
    return pl.pallas_call(
        _mix_ffn_kernel,
        grid=(s_len // bm, d_ff // tf),
        in_specs=[row(dm), row(attn.shape[1]), row(y.shape[1]), row(u.shape[1]),
                  const(d), const(wglu), const(bglu), const(wo_attn), const(wo_ssm), const(g2),
                  pl.BlockSpec((dm, tf), lambda i, f: (0, f)),
                  pl.BlockSpec((dm, tf), lambda i, f: (0, f)),
                  pl.BlockSpec((tf, dm), lambda i, f: (f, 0)),
                  const(gf)],
        out_specs=row(dm),
        out_shape=jax.ShapeDtypeStruct((s_len, dm), jnp.float32),
        scratch_shapes=[pltpu.VMEM((bm, dm), jnp.float32),
                        pltpu.VMEM((bm, dm), jnp.bfloat16),
                        pltpu.VMEM((bm, dm), jnp.float32)],
        compiler_params=pltpu.CompilerParams(dimension_semantics=("arbitrary", "arbitrary"),
                                             vmem_limit_bytes=V7X_VMEM_LIMIT),
        name="mix_ffn",
    )(x, attn, y, u, d, wglu, bglu, wo_attn, wo_ssm, g2, wg, wu, wd, gf)


def kernel(x, positions, norm1_g, w_in, lambda_q1, lambda_k1, lambda_q2, lambda_k2, subln_g, ssm_lambda_re, ssm_lambda_im, ssm_log_step, ssm_b_re, ssm_b_im, ssm_c_re, ssm_c_im, ssm_d, ssm_w_glu, ssm_b_glu, w_out, norm2_g, w_gate, w_up, w_down, final_g):
    bsz, s_len, d_model = x.shape
    assert bsz == 1 and norm1_g.shape[0] == 1, "one sequence, one layer"
    assert s_len % max(IN_PROJ_ROWS, FFN_ROWS, CHUNK * SCAN_BLOCK) == 0
    f32, bf16 = jnp.float32, jnp.bfloat16
    head_dim = lambda_q1.shape[-1]
    v_dim = subln_g.shape[-1]
    rot_dim = head_dim // 4
    n_groups, n_state, n_ch = ssm_b_re.shape[1:]
    ssm_width = n_groups * n_ch
    attn_width = d_model - ssm_width
    n_heads = attn_width // v_dim
    qk_width = n_heads * 2 * head_dim
    assert w_in.shape[-1] == 2 * qk_width + attn_width + ssm_width
    assert w_gate.shape[-1] % FFN_TILE == 0

    inv_freq = ROPE_THETA ** (-jnp.arange(0, rot_dim, 2, dtype=f32) / rot_dim)
    invf = jnp.concatenate([inv_freq, inv_freq]).reshape(rot_dim, 1)
    w = w_in[0]
    n_qkv = 2 * qk_width + attn_width
    qT, k, vT, u = _in_proj(x[0], positions[0], norm1_g[0].reshape(1, d_model).astype(f32), invf,
                            w[:, :n_qkv].T.astype(bf16), w[:, n_qkv:].astype(bf16),
                            n_heads=n_heads, head_dim=head_dim, rot_dim=rot_dim, v_dim=v_dim)

    lam_vec = lambda a: a[0].reshape(1, head_dim).astype(f32)
    attn = _attention(lam_vec(lambda_q1), lam_vec(lambda_k1), lam_vec(lambda_q2), lam_vec(lambda_k2),
                      subln_g[0].astype(f32), qT, k, vT, head_dim=head_dim)

    mats = _s5_matrices(ssm_lambda_re[0], ssm_lambda_im[0], ssm_log_step[0], ssm_b_re[0], ssm_b_im[0],
                        ssm_c_re[0], ssm_c_im[0])
    y = _s5_sequence(u, *mats)

    wo = w_out[0].astype(bf16)
    out = _mix_ffn(x[0], attn, y, u,
                   ssm_d[0].reshape(1, ssm_width).astype(f32), ssm_w_glu[0].astype(bf16),
                   ssm_b_glu[0].reshape(1, ssm_width).astype(f32), wo[:attn_width], wo[attn_width:],
                   norm2_g[0].reshape(1, d_model).astype(f32),
                   w_gate[0].astype(bf16), w_up[0].astype(bf16), w_down[0].astype(bf16),
                   final_g.reshape(1, d_model).astype(f32))
    return out[None]
```

```python
import functools
import math

import jax
import jax.numpy as jnp
from jax import lax
from jax.experimental import pallas as pl
from jax.experimental.pallas import tpu as pltpu

NORM_EPS = 1e-5
ROPE_THETA = 500000.0
LAMBDA_INIT = 0.8 - 0.6 * math.exp(-0.3 * 0)
LAM_RE_MAX = -1e-4
LOG2_E = math.log2(math.e)
V_EXT_ROWS = 16
CHUNK = 16
V7X_VMEM_LIMIT = 56 * 1024 * 1024
V7X_LANES = 128

IN_PROJ_ROWS = 512
ATTN_BQ = 256
ATTN_HEADS_PER_STEP = 2
ATTN_TICKS_PER_LOOP = 4
S5_ROWS = 2048
SCAN_BLOCK = 128
FFN_ROWS = 512
FFN_TILE = 256

_NT = (((1,), (1,)), ((), ()))


def _rms(x, g):
    return x * lax.rsqrt(jnp.mean(x * x, axis=-1, keepdims=True) + NORM_EPS) * g


def _in_proj_kernel(x_ref, pos_ref, g_ref, invf_ref, wT_ref, wu_ref,
                    qT_ref, k_ref, vT_ref, u_ref, *, n_maps, head_dim, rot_dim, qk_width):
    hn = _rms(x_ref[...], g_ref[...]).astype(jnp.bfloat16)
    pT = lax.dot_general(wT_ref[...], hn, _NT, preferred_element_type=jnp.float32)
    u_ref[...] = jnp.dot(hn, wu_ref[...], preferred_element_type=jnp.float32)

    ang = invf_ref[...] * pos_ref[0].astype(jnp.float32)
    cos, sin = jnp.cos(ang), jnp.sin(ang)
    half = rot_dim // 2
    c0, c1, s0, s1 = cos[:half], cos[half:], sin[:half], sin[half:]

    def rope(t, scale):
        rows = []
        for m in range(n_maps):
            r = m * head_dim
            x0, x1 = t[r:r + half], t[r + half:r + rot_dim]
            rows += [x0 * c0 - x1 * s0, x1 * c1 + x0 * s1, t[r + rot_dim:r + head_dim]]
        out = jnp.concatenate(rows, axis=0)
        return out * scale if scale != 1.0 else out

    qT_ref[...] = rope(pT[:qk_width], head_dim ** -0.5 * LOG2_E).astype(jnp.bfloat16)
    k_ref[...] = rope(pT[qk_width:2 * qk_width], 1.0).T.astype(jnp.bfloat16)
    n_heads, _, v_rows, bm = vT_ref.shape
    vT = pT[2 * qk_width:].astype(jnp.bfloat16).reshape(n_heads, v_rows - V_EXT_ROWS, bm)
    ext_row = lax.broadcasted_iota(jnp.int32, (n_heads, V_EXT_ROWS, bm), 1)
    ext = jnp.where(ext_row == 0, 1.0, 0.0).astype(jnp.bfloat16)
    vT_ref[...] = jnp.concatenate([vT, ext], axis=1).reshape(vT_ref.shape)


def _in_proj(x, pos, g, invf, wT, wu, *, n_heads, head_dim, rot_dim, v_dim):
    s_len, d = x.shape
    bm = IN_PROJ_ROWS
    qk_width = n_heads * 2 * head_dim
    v_width = n_heads * v_dim
    kern = functools.partial(_in_proj_kernel, n_maps=2 * n_heads, head_dim=head_dim,
                             rot_dim=rot_dim, qk_width=qk_width)
    const = lambda i: (0, 0)
    return pl.pallas_call(
        kern,
        grid=(s_len // bm,),
        in_specs=[pl.BlockSpec((bm, d), lambda i: (i, 0)),
                  pl.BlockSpec((1, 1, bm), lambda i: (i, 0, 0)),
                  pl.BlockSpec((1, d), const),
                  pl.BlockSpec((rot_dim, 1), const),
                  pl.BlockSpec(wT.shape, const),
                  pl.BlockSpec(wu.shape, const)],
        out_specs=[pl.BlockSpec((qk_width, bm), lambda i: (0, i)),
                   pl.BlockSpec((bm, qk_width), lambda i: (i, 0)),
                   pl.BlockSpec((n_heads, 1, v_dim + V_EXT_ROWS, bm), lambda i: (0, i, 0, 0)),
                   pl.BlockSpec((bm, wu.shape[1]), lambda i: (i, 0))],
        out_shape=[jax.ShapeDtypeStruct((qk_width, s_len), jnp.bfloat16),
                   jax.ShapeDtypeStruct((s_len, qk_width), jnp.bfloat16),
                   jax.ShapeDtypeStruct((n_heads, s_len // bm, v_dim + V_EXT_ROWS, bm), jnp.bfloat16),
                   jax.ShapeDtypeStruct((s_len, wu.shape[1]), jnp.float32)],
        compiler_params=pltpu.CompilerParams(dimension_semantics=("arbitrary",),
                                             vmem_limit_bytes=V7X_VMEM_LIMIT),
        name="in_proj",
    )(x, pos.reshape(s_len // bm, 1, bm), g, invf, wT, wu)


def _attn_kernel(lq1_ref, lk1_ref, lq2_ref, lk2_ref, g_ref, qT_ref, k_ref, vT_ref, o_ref, *scratch,
                 bq, bk, head_dim, v_dim, n_par):
    scratch = list(scratch)
    take = lambda n: [scratch.pop(0) for _ in range(n)]
    qbd_ref = take(n_par)
    s_ref, mb_ref, p_ref, al_ref = ([take(n_par) for _ in range(2)] for _ in range(4))
    m_ref, acc_ref = take(n_par), take(n_par)

    i = pl.program_id(1)
    kw = 2 * head_dim
    n_full = (i * bq) // bk

    zero = jnp.zeros((head_dim, bq), qT_ref.dtype)
    for h in range(n_par):
        qT = qT_ref[h * kw:(h + 1) * kw, :]
        qbd_ref[h][...] = jnp.concatenate([jnp.concatenate([qT[:head_dim], zero], axis=1),
                                           jnp.concatenate([zero, qT[head_dim:]], axis=1)], axis=0)
        m_ref[h][...] = jnp.full(m_ref[h].shape, -jnp.inf, jnp.float32)
        acc_ref[h][...] = jnp.zeros(acc_ref[h].shape, jnp.float32)
        for slot in range(2):
            p_ref[slot][h][...] = jnp.zeros(p_ref[slot][h].shape, p_ref[slot][h].dtype)
            al_ref[slot][h][...] = jnp.ones(al_ref[slot][h].shape, jnp.float32)

    def scores(t, h):
        kb = k_ref[pl.ds(pl.multiple_of(t * bk, bk), bk), h * kw:(h + 1) * kw]
        return jnp.dot(kb, qbd_ref[h][...], preferred_element_type=jnp.float32)

    def tick(t, *, s_rd, s_wr, p_rd, p_wr, qk, masked):
        for h in range(n_par):
            s = s_ref[s_rd][h][...]
            if masked:
                kpos = t * bk + lax.broadcasted_iota(jnp.int32, s.shape, 0)
                qcol = lax.broadcasted_iota(jnp.int32, s.shape, 1)
                qpos = i * bq + jnp.where(qcol >= bq, qcol - bq, qcol)
                s = jnp.where(kpos <= qpos, s, -jnp.inf)
                m_blk = jnp.max(s, axis=0, keepdims=True)
            else:
                m_blk = mb_ref[s_rd][h][...]
            m_old = m_ref[h][...]
            m_new = jnp.maximum(m_old, m_blk)
            p_ref[p_wr][h][...] = jnp.exp2(s - m_new).astype(p_ref[p_wr][h].dtype)
            al_ref[p_wr][h][...] = jnp.exp2(m_old - m_new)
            m_ref[h][...] = m_new
            pv = jnp.dot(vT_ref[h, jnp.maximum(t - 1, 0)], p_ref[p_rd][h][...],
                         preferred_element_type=jnp.float32)
            acc_ref[h][...] = al_ref[p_rd][h][...] * acc_ref[h][...] + pv
            if qk:
                s_next = scores(t + 1, h)
                s_ref[s_wr][h][...] = s_next
                mb_ref[s_wr][h][...] = jnp.max(s_next, axis=0, keepdims=True)

    even = dict(s_rd=0, s_wr=1, p_rd=1, p_wr=0)
    odd = dict(s_rd=1, s_wr=0, p_rd=0, p_wr=1)

    for h in range(n_par):
        s0 = scores(0, h)
        m0 = jnp.max(s0, axis=0, keepdims=True)
        for slot in range(2):
            s_ref[slot][h][...] = s0
            mb_ref[slot][h][...] = m0

    lead = n_full % ATTN_TICKS_PER_LOOP

    @pl.when(lead % 2 == 1)
    def _():
        tick(0, qk=True, masked=False, **odd)

    @pl.when(lead >= 2)
    def _():
        t = lead % 2
        tick(t, qk=True, masked=False, **even)
        tick(t + 1, qk=True, masked=False, **odd)

    def group(c, carry):
        t = lead + ATTN_TICKS_PER_LOOP * c
        for r in range(0, ATTN_TICKS_PER_LOOP, 2):
            tick(t + r, qk=True, masked=False, **even)
            tick(t + r + 1, qk=True, masked=False, **odd)
        return carry

    lax.fori_loop(0, n_full // ATTN_TICKS_PER_LOOP, group, 0)
    tick(n_full, qk=False, masked=True, **even)

    lam = (jnp.exp(jnp.sum(lq1_ref[...] * lk1_ref[...], axis=-1, keepdims=True))
           - jnp.exp(jnp.sum(lq2_ref[...] * lk2_ref[...], axis=-1, keepdims=True))
           + LAMBDA_INIT)
    last = even["p_wr"]
    for h in range(n_par):
        pv = jnp.dot(vT_ref[h, n_full], p_ref[last][h][...], preferred_element_type=jnp.float32)
        acc = al_ref[last][h][...] * acc_ref[h][...] + pv
        o = acc[:v_dim] / acc[v_dim:v_dim + 1]
        oT = o[:, :bq] - lam * o[:, bq:]
        y = oT * lax.rsqrt(jnp.mean(oT * oT, axis=0, keepdims=True) + NORM_EPS) * g_ref[...]
        o_ref[:, h * v_dim:(h + 1) * v_dim] = (y * (1.0 - LAMBDA_INIT)).T.astype(o_ref.dtype)


def _attention(lq1, lk1, lq2, lk2, subln_g, qT, k, vT, *, head_dim):
    n_heads, n_kblk, v_rows, bk = vT.shape
    v_dim = v_rows - V_EXT_ROWS
    s_len = k.shape[0]
    bq, n_par = ATTN_BQ, ATTN_HEADS_PER_STEP
    kern = functools.partial(_attn_kernel, bq=bq, bk=bk, head_dim=head_dim, v_dim=v_dim, n_par=n_par)
    vec = pl.BlockSpec((1, head_dim), lambda h, i: (0, 0))
    f32, bf16 = jnp.float32, jnp.bfloat16
    per_head = lambda shape, dtype, n=1: [pltpu.VMEM(shape, dtype) for _ in range(n * n_par)]
    scratch = (per_head((2 * head_dim, 2 * bq), bf16)
               + per_head((bk, 2 * bq), f32, 2)
               + per_head((1, 2 * bq), f32, 2)
               + per_head((bk, 2 * bq), bf16, 2)
               + per_head((1, 2 * bq), f32, 2)
               + per_head((1, 2 * bq), f32)
               + per_head((v_rows, 2 * bq), f32))
    return pl.pallas_call(
        kern,
        grid=(n_heads // n_par, s_len // bq),
        in_specs=[vec, vec, vec, vec,
                  pl.BlockSpec((v_dim, 1), lambda h, i: (0, 0)),
                  pl.BlockSpec((n_par * 2 * head_dim, bq), lambda h, i: (h, i)),
                  pl.BlockSpec((s_len, n_par * 2 * head_dim), lambda h, i: (0, h)),
                  pl.BlockSpec((n_par, n_kblk, v_rows, bk), lambda h, i: (h, 0, 0, 0))],
        out_specs=pl.BlockSpec((bq, n_par * v_dim), lambda h, i: (i, h)),
        out_shape=jax.ShapeDtypeStruct((s_len, n_heads * v_dim), bf16),
        scratch_shapes=scratch,
        compiler_params=pltpu.CompilerParams(dimension_semantics=("arbitrary", "arbitrary"),
                                             vmem_limit_bytes=V7X_VMEM_LIMIT),
        name="attn",
    )(lq1, lk1, lq2, lk2, subln_g.reshape(v_dim, 1), qT, k, vT)


def _s5_matrices(lam_re, lam_im, log_step, b_re, b_im, c_re, c_im):
    f32 = jnp.float32
    hi = lax.Precision.HIGHEST
    n_groups, n_state, n_ch = b_re.shape
    lr = jnp.minimum(lam_re.astype(f32), LAM_RE_MAX)
    li = lam_im.astype(f32)
    step = jnp.exp(log_step.astype(f32))[:, None]
    mag = jnp.exp(lr * step)
    lb_re, lb_im = mag * jnp.cos(li * step), mag * jnp.sin(li * step)
    denom = lr * lr + li * li
    nr, ni = lb_re - 1.0, lb_im
    coef_re = (nr * lr + ni * li) / denom
    coef_im = (ni * lr - nr * li) / denom
    br, bi = b_re.astype(f32), b_im.astype(f32)
    bb_re = coef_re[..., None] * br - coef_im[..., None] * bi
    bb_im = coef_re[..., None] * bi + coef_im[..., None] * br
    tau = jnp.arange(CHUNK + 1, dtype=f32)[None, :, None]
    pmag = jnp.exp(lr[:, None, :] * step[:, None, :] * tau)
    pw_re = pmag * jnp.cos(li[:, None, :] * step[:, None, :] * tau)
    pw_im = pmag * jnp.sin(li[:, None, :] * step[:, None, :] * tau)
    cr, ci = c_re.astype(f32)[:, None], c_im.astype(f32)[:, None]
    cp_re = cr * pw_re[:, :, None, :] - ci * pw_im[:, :, None, :]
    cp_im = cr * pw_im[:, :, None, :] + ci * pw_re[:, :, None, :]
    kmat = (jnp.einsum('gthp,gpk->gthk', cp_re, bb_re, precision=hi)
            - jnp.einsum('gthp,gpk->gthk', cp_im, bb_im, precision=hi))
    j = jnp.arange(CHUNK)[:, None]
    t = jnp.arange(CHUNK)[None, :]
    lag = t - j
    ksel = jnp.where((lag >= 0)[None, :, :, None, None], kmat[:, jnp.maximum(lag, 0)], 0.0)
    tmat = ksel.transpose(0, 1, 4, 2, 3).reshape(n_groups, CHUNK * n_ch, CHUNK * n_ch)
    rev = CHUNK - 1 - jnp.arange(CHUNK)
    ar, ai = pw_re[:, rev][:, :, None, :], pw_im[:, rev][:, :, None, :]
    brt, bit = bb_re.transpose(0, 2, 1)[:, None], bb_im.transpose(0, 2, 1)[:, None]
    w_re = (ar * brt - ai * bit).reshape(n_groups, CHUNK * n_ch, n_state)
    w_im = (ar * bit + ai * brt).reshape(n_groups, CHUNK * n_ch, n_state)
    wmat = jnp.concatenate([w_re, w_im, w_im, w_re], axis=-1)
    e_re = cp_re[:, 1:].transpose(0, 3, 1, 2).reshape(n_groups, n_state, CHUNK * n_ch)
    e_im = cp_im[:, 1:].transpose(0, 3, 1, 2).reshape(n_groups, n_state, CHUNK * n_ch)
    emat = jnp.concatenate([e_re, -e_im], axis=1)
    al_re, al_im = pw_re[:, CHUNK], pw_im[:, CHUNK]
    a1 = jnp.concatenate([al_re, al_re], axis=-1)
    a2 = jnp.concatenate([-al_im, al_im], axis=-1)
    bf = jnp.bfloat16
    return tmat.astype(bf), wmat.astype(bf), emat.astype(bf), a1, a2, -a2


def _chunk_rows_to_lanes(u_refs, n_cb, n_ch):
    lanes = u_refs[0].shape[1]
    gps = lanes // n_ch
    piece = lax.broadcasted_iota(jnp.int32, (n_cb, lanes), 1) // n_ch
    out = []
    for u_ref in u_refs:
        rows = [u_ref[pl.ds(j, n_cb, stride=CHUNK), :] for j in range(CHUNK)]
        for gl in range(gps):
            cols = []
            for m in range(CHUNK // gps):
                acc = None
                for r in range(gps):
                    x = rows[gps * m + r]
                    shift = ((r - gl) * n_ch) % lanes
                    x = x if shift == 0 else pltpu.roll(x, shift, axis=1)
                    acc = x if acc is None else jnp.where(piece == r, x, acc)
                cols.append(acc)
            out.append(jnp.concatenate(cols, axis=1))
    return out


def _chunk_lanes_to_rows(ys, o_refs, n_cb, n_ch):
    lanes = o_refs[0].shape[1]
    gps = lanes // n_ch
    piece = lax.broadcasted_iota(jnp.int32, (n_cb, lanes), 1) // n_ch
    for slab, o_ref in enumerate(o_refs):
        for t in range(CHUNK):
            m, r = divmod(t, gps)
            acc = None
            for gl in range(gps):
                y = ys[slab * gps + gl][:, m * lanes:(m + 1) * lanes]
                shift = ((gl - r) * n_ch) % lanes
                y = y if shift == 0 else pltpu.roll(y, shift, axis=1)
                acc = y if acc is None else jnp.where(piece == gl, y, acc)
            o_ref[pl.ds(t, n_cb, stride=CHUNK), :] = acc


def _s5_state_kernel(*refs, n_slabs, n_ch):
    u_refs, (w_ref, uf_ref, s_ref) = refs[:n_slabs], refs[n_slabs:]
    flat = _chunk_rows_to_lanes(u_refs, uf_ref.shape[1], n_ch)
    for g, f in enumerate(flat):
        f = f.astype(jnp.bfloat16)
        uf_ref[g] = f
        s_ref[g] = jnp.dot(f, w_ref[g], preferred_element_type=jnp.float32)


def _s5_scan_kernel(s_ref, a1_ref, a2_ref, a2s_ref, x0_ref, x_ref, xs_ref, *, n_state):
    @pl.when(pl.program_id(0) == 0)
    def _():
        x_ref[...] = jnp.zeros(x_ref.shape, jnp.float32)
        xs_ref[...] = jnp.zeros(xs_ref.shape, jnp.float32)

    a1, a2, a2s = a1_ref[...], a2_ref[...], a2s_ref[...]
    w = 2 * n_state

    def body(c, carry):
        x, xs = carry
        x0_ref[c] = x
        s = s_ref[c]
        return a1 * x + a2 * xs + s[:, :w], a1 * xs + a2s * x + s[:, w:]

    x, xs = lax.fori_loop(0, s_ref.shape[0], body, (x_ref[...], xs_ref[...]), unroll=4)
    x_ref[...] = x
    xs_ref[...] = xs


def _s5_out_kernel(*refs, n_slabs, n_ch):
    u_refs = refs[:n_slabs]
    uf_ref, x0_ref, t_ref, e_ref, d_ref, wglu_ref, bglu_ref, o_ref = refs[n_slabs:n_slabs + 8]
    y_refs = refs[n_slabs + 8:]
    ys = [jnp.dot(uf_ref[g], t_ref[g], preferred_element_type=jnp.float32)
          + jnp.dot(x0_ref[g], e_ref[g], preferred_element_type=jnp.float32)
          for g in range(uf_ref.shape[0])]
    _chunk_lanes_to_rows(ys, y_refs, uf_ref.shape[1], n_ch)
    y = jnp.concatenate([y_ref[...] for y_ref in y_refs], axis=1)
    u = jnp.concatenate([u_ref[...] for u_ref in u_refs], axis=1)
    y = jax.nn.gelu(y + d_ref[...] * u)
    gate = jnp.dot(y.astype(jnp.bfloat16), wglu_ref[...],
                   preferred_element_type=jnp.float32) + bglu_ref[...]
    o_ref[...] = (y * jax.nn.sigmoid(gate)).astype(o_ref.dtype)


def _s5_mixer(u, tmat, wmat, emat, a1, a2, a2s, d, wglu, bglu):
    s_len, width = u.shape
    n_groups, cw, _ = tmat.shape
    n_ch = cw // CHUNK
    n_state = a1.shape[1] // 2
    n_chunks = s_len // CHUNK
    rows = S5_ROWS
    n_cb = rows // CHUNK
    lanes = V7X_LANES
    n_slabs = width // lanes
    params = pltpu.CompilerParams(dimension_semantics=("arbitrary",), vmem_limit_bytes=V7X_VMEM_LIMIT)
    slab_specs = [pl.BlockSpec((rows, lanes), functools.partial(lambda k, i: (i, k), k))
                  for k in range(n_slabs)]
    const = lambda a: pl.BlockSpec(a.shape, lambda i: (0,) * a.ndim, pipeline_mode=pl.Buffered(1))
    blk = lambda w: pl.BlockSpec((n_groups, n_cb, w), lambda i: (0, i, 0))
    u_flat, s_loc = pl.pallas_call(
        functools.partial(_s5_state_kernel, n_slabs=n_slabs, n_ch=n_ch),
        grid=(s_len // rows,),
        in_specs=slab_specs + [const(wmat)],
        out_specs=[blk(cw), blk(4 * n_state)],
        out_shape=[jax.ShapeDtypeStruct((n_groups, n_chunks, cw), jnp.bfloat16),
                   jax.ShapeDtypeStruct((n_groups, n_chunks, 4 * n_state), jnp.float32)],
        compiler_params=params, name="s5_state",
    )(*([u] * n_slabs), wmat)
    cb = SCAN_BLOCK
    x0 = pl.pallas_call(
        functools.partial(_s5_scan_kernel, n_state=n_state),
        grid=(n_chunks // cb,),
        in_specs=[pl.BlockSpec((cb, n_groups, 4 * n_state), lambda i: (i, 0, 0)),
                  pl.BlockSpec(a1.shape, lambda i: (0, 0)),
                  pl.BlockSpec(a2.shape, lambda i: (0, 0)),
                  pl.BlockSpec(a2s.shape, lambda i: (0, 0))],
        out_specs=pl.BlockSpec((cb, n_groups, 2 * n_state), lambda i: (i, 0, 0)),
        out_shape=jax.ShapeDtypeStruct((n_chunks, n_groups, 2 * n_state), jnp.float32),
        scratch_shapes=[pltpu.VMEM(a1.shape, jnp.float32), pltpu.VMEM(a1.shape, jnp.float32)],
        compiler_params=params, name="s5_scan",
    )(s_loc.transpose(1, 0, 2), a1, a2, a2s)
    return pl.pallas_call(
        functools.partial(_s5_out_kernel, n_slabs=n_slabs, n_ch=n_ch),
        grid=(s_len // rows,),
        in_specs=slab_specs + [blk(cw), blk(2 * n_state), const(tmat), const(emat),
                               const(d), const(wglu), const(bglu)],
        out_specs=pl.BlockSpec((rows, width), lambda i: (i, 0)),
        out_shape=jax.ShapeDtypeStruct((s_len, width), jnp.bfloat16),
        scratch_shapes=[pltpu.VMEM((rows, lanes), jnp.float32) for _ in range(n_slabs)],
        compiler_params=params, name="s5_out",
    )(*([u] * n_slabs), u_flat, x0.transpose(1, 0, 2).astype(jnp.bfloat16), tmat, emat, d, wglu, bglu)


def _mix_ffn_kernel(x_ref, attn_ref, ssm_ref, woa_ref, wos_ref, g2_ref, wg_ref, wu_ref, wd_ref, gf_ref,
                    o_ref, act_ref, *, tf):
    mixed = (jnp.dot(attn_ref[...], woa_ref[...], preferred_element_type=jnp.float32)
             + jnp.dot(ssm_ref[...], wos_ref[...], preferred_element_type=jnp.float32))
    h = x_ref[...] + mixed
    hn = _rms(h, g2_ref[...]).astype(jnp.bfloat16)
    for f in range(0, wg_ref.shape[1], tf):
        gate = jnp.dot(hn, wg_ref[:, f:f + tf], preferred_element_type=jnp.float32)
        up = jnp.dot(hn, wu_ref[:, f:f + tf], preferred_element_type=jnp.float32)
        act_ref[:, f:f + tf] = (jax.nn.silu(gate) * up).astype(jnp.bfloat16)
    ffn = jnp.dot(act_ref[...], wd_ref[...], preferred_element_type=jnp.float32)
    o_ref[...] = _rms(h + ffn, gf_ref[...])


def _mix_ffn(x, attn, ssm, wo_attn, wo_ssm, g2, wg, wu, wd, gf):
    s_len, dm = x.shape
    bm = FFN_ROWS
    d_ff = wg.shape[1]
    row = lambda w: pl.BlockSpec((bm, w), lambda i: (i, 0))
    const = lambda a: pl.BlockSpec(a.shape, lambda i: (0, 0), pipeline_mode=pl.Buffered(1))
    return pl.pallas_call(
        functools.partial(_mix_ffn_kernel, tf=FFN_TILE),
        grid=(s_len // bm,),
        in_specs=[row(dm), row(attn.shape[1]), row(ssm.shape[1]),
                  const(wo_attn), const(wo_ssm), const(g2), const(wg), const(wu), const(wd), const(gf)],
        out_specs=row(dm),
        out_shape=jax.ShapeDtypeStruct((s_len, dm), jnp.float32),
        scratch_shapes=[pltpu.VMEM((bm, d_ff), jnp.bfloat16)],
        compiler_params=pltpu.CompilerParams(dimension_semantics=("arbitrary",),
                                             vmem_limit_bytes=V7X_VMEM_LIMIT),
        name="mix_ffn",
    )(x, attn, ssm, wo_attn, wo_ssm, g2, wg, wu, wd, gf)


def kernel(x, positions, norm1_g, w_in, lambda_q1, lambda_k1, lambda_q2, lambda_k2, subln_g, ssm_lambda_re, ssm_lambda_im, ssm_log_step, ssm_b_re, ssm_b_im, ssm_c_re, ssm_c_im, ssm_d, ssm_w_glu, ssm_b_glu, w_out, norm2_g, w_gate, w_up, w_down, final_g):
    bsz, s_len, d_model = x.shape
    assert bsz == 1 and norm1_g.shape[0] == 1, "one sequence, one layer"
    assert s_len % max(IN_PROJ_ROWS, FFN_ROWS, S5_ROWS, CHUNK * SCAN_BLOCK) == 0
    f32, bf16 = jnp.float32, jnp.bfloat16
    head_dim = lambda_q1.shape[-1]
    v_dim = subln_g.shape[-1]
    rot_dim = head_dim // 4
    n_groups, n_state, n_ch = ssm_b_re.shape[1:]
    ssm_width = n_groups * n_ch
    attn_width = d_model - ssm_width
    n_heads = attn_width // v_dim
    qk_width = n_heads * 2 * head_dim
    assert w_in.shape[-1] == 2 * qk_width + attn_width + ssm_width
    assert w_gate.shape[-1] % FFN_TILE == 0

    inv_freq = ROPE_THETA ** (-jnp.arange(0, rot_dim, 2, dtype=f32) / rot_dim)
    invf = jnp.concatenate([inv_freq, inv_freq]).reshape(rot_dim, 1)
    w = w_in[0]
    n_qkv = 2 * qk_width + attn_width
    qT, k, vT, u = _in_proj(x[0], positions[0], norm1_g[0].reshape(1, d_model).astype(f32), invf,
                            w[:, :n_qkv].T.astype(bf16), w[:, n_qkv:].astype(bf16),
                            n_heads=n_heads, head_dim=head_dim, rot_dim=rot_dim, v_dim=v_dim)

    lam_vec = lambda a: a[0].reshape(1, head_dim).astype(f32)
    attn = _attention(lam_vec(lambda_q1), lam_vec(lambda_k1), lam_vec(lambda_q2), lam_vec(lambda_k2),
                      subln_g[0].astype(f32), qT, k, vT, head_dim=head_dim)

    mats = _s5_matrices(ssm_lambda_re[0], ssm_lambda_im[0], ssm_log_step[0], ssm_b_re[0], ssm_b_im[0],
                        ssm_c_re[0], ssm_c_im[0])
    ssm = _s5_mixer(u, *mats, ssm_d[0].reshape(1, ssm_width).astype(f32), ssm_w_glu[0].astype(bf16),
                    ssm_b_glu[0].reshape(1, ssm_width).astype(f32))

    wo = w_out[0].astype(bf16)
    out = _mix_ffn(x[0], attn, ssm, wo[:attn_width], wo[attn_width:],
                   norm2_g[0].reshape(1, d_model).astype(f32),
                   w_gate[0].astype(bf16), w_up[0].astype(bf16), w_down[0].astype(bf16),
                   final_g.reshape(1, d_model).astype(f32))
    return out[None]
```

```python
import functools
import math

import jax
import jax.numpy as jnp
from jax import lax
from jax.experimental import pallas as pl
from jax.experimental.pallas import tpu as pltpu

NORM_EPS = 1e-5
ROPE_THETA = 500000.0
LAMBDA_INIT = 0.8 - 0.6 * math.exp(-0.3 * 0)
LAM_RE_MAX = -1e-4
LOG2_E = math.log2(math.e)
V_EXT_ROWS = 16
CHUNK = 16
V7X_VMEM_LIMIT = 56 * 1024 * 1024
V7X_LANES = 128

IN_PROJ_ROWS = 512
ATTN_BQ = 256
ATTN_HEADS_PER_STEP = 2
ATTN_TICKS_PER_LOOP = 4
ATTN_ROW_CHUNK = 64
S5_ROWS = 2048
SCAN_BLOCK = 128
FFN_ROWS = 512
FFN_TILE = 256

_NT = (((1,), (1,)), ((), ()))


def _rms(x, g):
    return x * lax.rsqrt(jnp.mean(x * x, axis=-1, keepdims=True) + NORM_EPS) * g


def _in_proj_kernel(x_ref, pos_ref, g_ref, invf_ref, wT_ref, wu_ref,
                    qT_ref, k_ref, vT_ref, u_ref, *, n_maps, head_dim, rot_dim, qk_width):
    hn = _rms(x_ref[...], g_ref[...]).astype(jnp.bfloat16)
    pT = lax.dot_general(wT_ref[...], hn, _NT, preferred_element_type=jnp.float32)
    u_ref[...] = jnp.dot(hn, wu_ref[...], preferred_element_type=jnp.float32)

    ang = invf_ref[...] * pos_ref[0].astype(jnp.float32)
    cos, sin = jnp.cos(ang), jnp.sin(ang)
    half = rot_dim // 2
    c0, c1, s0, s1 = cos[:half], cos[half:], sin[:half], sin[half:]

    def rope(t, scale):
        rows = []
        for m in range(n_maps):
            r = m * head_dim
            x0, x1 = t[r:r + half], t[r + half:r + rot_dim]
            rows += [x0 * c0 - x1 * s0, x1 * c1 + x0 * s1, t[r + rot_dim:r + head_dim]]
        out = jnp.concatenate(rows, axis=0)
        return out * scale if scale != 1.0 else out

    qT_ref[...] = rope(pT[:qk_width], head_dim ** -0.5 * LOG2_E).astype(jnp.bfloat16)
    n_heads, _, v_rows, bm = vT_ref.shape
    kT = rope(pT[qk_width:2 * qk_width], 1.0)
    for h in range(n_heads):
        k_ref[h] = kT[h * 2 * head_dim:(h + 1) * 2 * head_dim].T.astype(jnp.bfloat16)
    vT = pT[2 * qk_width:].astype(jnp.bfloat16).reshape(n_heads, v_rows - V_EXT_ROWS, bm)
    ext_row = lax.broadcasted_iota(jnp.int32, (n_heads, V_EXT_ROWS, bm), 1)
    ext = jnp.where(ext_row == 0, 1.0, 0.0).astype(jnp.bfloat16)
    vT_ref[...] = jnp.concatenate([vT, ext], axis=1).reshape(vT_ref.shape)


def _in_proj(x, pos, g, invf, wT, wu, *, n_heads, head_dim, rot_dim, v_dim):
    s_len, d = x.shape
    bm = IN_PROJ_ROWS
    qk_width = n_heads * 2 * head_dim
    v_width = n_heads * v_dim
    kern = functools.partial(_in_proj_kernel, n_maps=2 * n_heads, head_dim=head_dim,
                             rot_dim=rot_dim, qk_width=qk_width)
    const = lambda i: (0, 0)
    return pl.pallas_call(
        kern,
        grid=(s_len // bm,),
        in_specs=[pl.BlockSpec((bm, d), lambda i: (i, 0)),
                  pl.BlockSpec((1, 1, bm), lambda i: (i, 0, 0)),
                  pl.BlockSpec((1, d), const),
                  pl.BlockSpec((rot_dim, 1), const),
                  pl.BlockSpec(wT.shape, const),
                  pl.BlockSpec(wu.shape, const)],
        out_specs=[pl.BlockSpec((qk_width, bm), lambda i: (0, i)),
                   pl.BlockSpec((n_heads, bm, 2 * head_dim), lambda i: (0, i, 0)),
                   pl.BlockSpec((n_heads, 1, v_dim + V_EXT_ROWS, bm), lambda i: (0, i, 0, 0)),
                   pl.BlockSpec((bm, wu.shape[1]), lambda i: (i, 0))],
        out_shape=[jax.ShapeDtypeStruct((qk_width, s_len), jnp.bfloat16),
                   jax.ShapeDtypeStruct((n_heads, s_len, 2 * head_dim), jnp.bfloat16),
                   jax.ShapeDtypeStruct((n_heads, s_len // bm, v_dim + V_EXT_ROWS, bm), jnp.bfloat16),
                   jax.ShapeDtypeStruct((s_len, wu.shape[1]), jnp.float32)],
        compiler_params=pltpu.CompilerParams(dimension_semantics=("arbitrary",),
                                             vmem_limit_bytes=V7X_VMEM_LIMIT),
        name="in_proj",
    )(x, pos.reshape(s_len // bm, 1, bm), g, invf, wT, wu)


def _attn_kernel(lq1_ref, lk1_ref, lq2_ref, lk2_ref, g_ref, qT_ref, k_ref, vT_ref, o_ref, *scratch,
                 bq, bk, head_dim, v_dim, n_par):
    scratch = list(scratch)
    take = lambda n: [scratch.pop(0) for _ in range(n)]
    qbd_ref = take(n_par)
    s_ref, mb_ref, p_ref, al_ref = ([take(n_par) for _ in range(2)] for _ in range(4))
    m_ref, acc_ref = take(n_par), take(n_par)

    i = pl.program_id(1)
    kw = 2 * head_dim
    n_full = (i * bq) // bk

    zero = jnp.zeros((head_dim, bq), qT_ref.dtype)
    for h in range(n_par):
        qT = qT_ref[h * kw:(h + 1) * kw, :]
        qbd_ref[h][...] = jnp.concatenate([jnp.concatenate([qT[:head_dim], zero], axis=1),
                                           jnp.concatenate([zero, qT[head_dim:]], axis=1)], axis=0)
        m_ref[h][...] = jnp.full(m_ref[h].shape, -jnp.inf, jnp.float32)
        acc_ref[h][...] = jnp.zeros(acc_ref[h].shape, jnp.float32)
        for slot in range(2):
            p_ref[slot][h][...] = jnp.zeros(p_ref[slot][h].shape, p_ref[slot][h].dtype)
            al_ref[slot][h][...] = jnp.ones(al_ref[slot][h].shape, jnp.float32)

    def scores(t, h):
        kb = k_ref[h, pl.ds(pl.multiple_of(t * bk, bk), bk), :]
        return jnp.dot(kb, qbd_ref[h][...], preferred_element_type=jnp.float32)

    row_chunks = range(0, bk, ATTN_ROW_CHUNK)

    def store_scores(s, h, slots):
        m8 = None
        for r in row_chunks:
            c = s[r:r + ATTN_ROW_CHUNK]
            for slot in slots:
                s_ref[slot][h][r:r + ATTN_ROW_CHUNK, :] = c
            c8 = jnp.max(c.reshape(ATTN_ROW_CHUNK // 8, 8, c.shape[1]), axis=0)
            m8 = c8 if m8 is None else jnp.maximum(m8, c8)
        m = jnp.max(m8, axis=0, keepdims=True)
        for slot in slots:
            mb_ref[slot][h][...] = m

    def tick(t, *, s_rd, s_wr, p_rd, p_wr, qk, masked):
        for h in range(n_par):
            m_old = m_ref[h][...]
            if masked:
                s = s_ref[s_rd][h][...]
                kpos = t * bk + lax.broadcasted_iota(jnp.int32, s.shape, 0)
                qcol = lax.broadcasted_iota(jnp.int32, s.shape, 1)
                qpos = i * bq + jnp.where(qcol >= bq, qcol - bq, qcol)
                s = jnp.where(kpos <= qpos, s, -jnp.inf)
                m_new = jnp.maximum(m_old, jnp.max(s, axis=0, keepdims=True))
                p_ref[p_wr][h][...] = jnp.exp2(s - m_new).astype(p_ref[p_wr][h].dtype)
            else:
                m_new = jnp.maximum(m_old, mb_ref[s_rd][h][...])
                for r in row_chunks:
                    c = s_ref[s_rd][h][r:r + ATTN_ROW_CHUNK, :]
                    p_ref[p_wr][h][r:r + ATTN_ROW_CHUNK, :] = jnp.exp2(c - m_new).astype(p_ref[p_wr][h].dtype)
            al_ref[p_wr][h][...] = jnp.exp2(m_old - m_new)
            m_ref[h][...] = m_new
            pv = jnp.dot(vT_ref[h, jnp.maximum(t - 1, 0)], p_ref[p_rd][h][...],
                         preferred_element_type=jnp.float32)
            acc_ref[h][...] = al_ref[p_rd][h][...] * acc_ref[h][...] + pv
            if qk:
                store_scores(scores(t + 1, h), h, (s_wr,))

    even = dict(s_rd=0, s_wr=1, p_rd=1, p_wr=0)
    odd = dict(s_rd=1, s_wr=0, p_rd=0, p_wr=1)

    for h in range(n_par):
        store_scores(scores(0, h), h, (0, 1))

    lead = n_full % ATTN_TICKS_PER_LOOP

    @pl.when(lead % 2 == 1)
    def _():
        tick(0, qk=True, masked=False, **odd)

    @pl.when(lead >= 2)
    def _():
        t = lead % 2
        tick(t, qk=True, masked=False, **even)
        tick(t + 1, qk=True, masked=False, **odd)

    def group(c, carry):
        t = lead + ATTN_TICKS_PER_LOOP * c
        for r in range(0, ATTN_TICKS_PER_LOOP, 2):
            tick(t + r, qk=True, masked=False, **even)
            tick(t + r + 1, qk=True, masked=False, **odd)
        return carry

    lax.fori_loop(0, n_full // ATTN_TICKS_PER_LOOP, group, 0)
    tick(n_full, qk=False, masked=True, **even)

    lam = (jnp.exp(jnp.sum(lq1_ref[...] * lk1_ref[...], axis=-1, keepdims=True))
           - jnp.exp(jnp.sum(lq2_ref[...] * lk2_ref[...], axis=-1, keepdims=True))
           + LAMBDA_INIT)
    last = even["p_wr"]
    for h in range(n_par):
        pv = jnp.dot(vT_ref[h, n_full], p_ref[last][h][...], preferred_element_type=jnp.float32)
        acc = al_ref[last][h][...] * acc_ref[h][...] + pv
        o = acc[:v_dim] / acc[v_dim:v_dim + 1]
        oT = o[:, :bq] - lam * o[:, bq:]
        y = oT * lax.rsqrt(jnp.mean(oT * oT, axis=0, keepdims=True) + NORM_EPS) * g_ref[...]
        o_ref[:, h * v_dim:(h + 1) * v_dim] = (y * (1.0 - LAMBDA_INIT)).T.astype(o_ref.dtype)


def _attention(lq1, lk1, lq2, lk2, subln_g, qT, k, vT, *, head_dim):
    n_heads, n_kblk, v_rows, bk = vT.shape
    v_dim = v_rows - V_EXT_ROWS
    s_len = k.shape[1]
    bq, n_par = ATTN_BQ, ATTN_HEADS_PER_STEP
    kern = functools.partial(_attn_kernel, bq=bq, bk=bk, head_dim=head_dim, v_dim=v_dim, n_par=n_par)
    vec = pl.BlockSpec((1, head_dim), lambda h, i: (0, 0))
    f32, bf16 = jnp.float32, jnp.bfloat16
    per_head = lambda shape, dtype, n=1: [pltpu.VMEM(shape, dtype) for _ in range(n * n_par)]
    scratch = (per_head((2 * head_dim, 2 * bq), bf16)
               + per_head((bk, 2 * bq), f32, 2)
               + per_head((1, 2 * bq), f32, 2)
               + per_head((bk, 2 * bq), bf16, 2)
               + per_head((1, 2 * bq), f32, 2)
               + per_head((1, 2 * bq), f32)
               + per_head((v_rows, 2 * bq), f32))
    return pl.pallas_call(
        kern,
        grid=(n_heads // n_par, s_len // bq),
        in_specs=[vec, vec, vec, vec,
                  pl.BlockSpec((v_dim, 1), lambda h, i: (0, 0)),
                  pl.BlockSpec((n_par * 2 * head_dim, bq), lambda h, i: (h, i)),
                  pl.BlockSpec((n_par, s_len, 2 * head_dim), lambda h, i: (h, 0, 0)),
                  pl.BlockSpec((n_par, n_kblk, v_rows, bk), lambda h, i: (h, 0, 0, 0))],
        out_specs=pl.BlockSpec((bq, n_par * v_dim), lambda h, i: (i, h)),
        out_shape=jax.ShapeDtypeStruct((s_len, n_heads * v_dim), bf16),
        scratch_shapes=scratch,
        compiler_params=pltpu.CompilerParams(dimension_semantics=("arbitrary", "arbitrary"),
                                             vmem_limit_bytes=V7X_VMEM_LIMIT),
        name="attn",
    )(lq1, lk1, lq2, lk2, subln_g.reshape(v_dim, 1), qT, k, vT)


def _s5_matrices(lam_re, lam_im, log_step, b_re, b_im, c_re, c_im):
    f32 = jnp.float32
    n_groups, n_state, n_ch = b_re.shape
    lr = jnp.minimum(lam_re.astype(f32), LAM_RE_MAX)
    li = lam_im.astype(f32)
    step = jnp.exp(log_step.astype(f32))[:, None]
    mag = jnp.exp(lr * step)
    lb_re, lb_im = mag * jnp.cos(li * step), mag * jnp.sin(li * step)
    denom = lr * lr + li * li
    nr, ni = lb_re - 1.0, lb_im
    coef_re = (nr * lr + ni * li) / denom
    coef_im = (ni * lr - nr * li) / denom
    br, bi = b_re.astype(f32), b_im.astype(f32)
    bb_re = coef_re[..., None] * br - coef_im[..., None] * bi
    bb_im = coef_re[..., None] * bi + coef_im[..., None] * br
    tau = jnp.arange(CHUNK + 1, dtype=f32)[None, :, None]
    pmag = jnp.exp(lr[:, None, :] * step[:, None, :] * tau)
    pw_re = pmag * jnp.cos(li[:, None, :] * step[:, None, :] * tau)
    pw_im = pmag * jnp.sin(li[:, None, :] * step[:, None, :] * tau)
    cr, ci = c_re.astype(f32)[:, None], c_im.astype(f32)[:, None]
    cp_re = cr * pw_re[:, :, None, :] - ci * pw_im[:, :, None, :]
    cp_im = cr * pw_im[:, :, None, :] + ci * pw_re[:, :, None, :]
    kmat = jnp.sum(cp_re[..., None] * bb_re[:, None, None] - cp_im[..., None] * bb_im[:, None, None], axis=3)
    kt = kmat[:, :CHUNK].transpose(0, 1, 3, 2)
    j = jnp.arange(CHUNK)[:, None]
    t = jnp.arange(CHUNK)[None, :]
    at_lag = (t - j)[None] == jnp.arange(CHUNK)[:, None, None]
    tmat = jnp.sum(jnp.where(at_lag[None, :, :, None, :, None], kt[:, :, None, :, None, :], 0.0),
                   axis=1).reshape(n_groups, CHUNK * n_ch, CHUNK * n_ch)
    rev = CHUNK - 1 - jnp.arange(CHUNK)
    ar, ai = pw_re[:, rev][:, :, None, :], pw_im[:, rev][:, :, None, :]
    brt, bit = bb_re.transpose(0, 2, 1)[:, None], bb_im.transpose(0, 2, 1)[:, None]
    w_re = (ar * brt - ai * bit).reshape(n_groups, CHUNK * n_ch, n_state)
    w_im = (ar * bit + ai * brt).reshape(n_groups, CHUNK * n_ch, n_state)
    wmat = jnp.concatenate([w_re, w_im, w_im, w_re], axis=-1)
    e_re = cp_re[:, 1:].transpose(0, 3, 1, 2).reshape(n_groups, n_state, CHUNK * n_ch)
    e_im = cp_im[:, 1:].transpose(0, 3, 1, 2).reshape(n_groups, n_state, CHUNK * n_ch)
    emat = jnp.concatenate([e_re, -e_im], axis=1)
    al_re, al_im = pw_re[:, CHUNK], pw_im[:, CHUNK]
    a1 = jnp.concatenate([al_re, al_re], axis=-1)
    a2 = jnp.concatenate([-al_im, al_im], axis=-1)
    bf = jnp.bfloat16
    return tmat.astype(bf), wmat.astype(bf), emat.astype(bf), a1, a2, -a2


def _chunk_rows_to_lanes(u_refs, n_cb, n_ch):
    lanes = u_refs[0].shape[1]
    gps = lanes // n_ch
    piece = lax.broadcasted_iota(jnp.int32, (n_cb, lanes), 1) // n_ch
    out = []
    for u_ref in u_refs:
        rows = [u_ref[pl.ds(j, n_cb, stride=CHUNK), :] for j in range(CHUNK)]
        for gl in range(gps):
            cols = []
            for m in range(CHUNK // gps):
                acc = None
                for r in range(gps):
                    x = rows[gps * m + r]
                    shift = ((r - gl) * n_ch) % lanes
                    x = x if shift == 0 else pltpu.roll(x, shift, axis=1)
                    acc = x if acc is None else jnp.where(piece == r, x, acc)
                cols.append(acc)
            out.append(jnp.concatenate(cols, axis=1))
    return out


def _chunk_lanes_to_rows(ys, o_refs, n_cb, n_ch):
    lanes = o_refs[0].shape[1]
    gps = lanes // n_ch
    piece = lax.broadcasted_iota(jnp.int32, (n_cb, lanes), 1) // n_ch
    for slab, o_ref in enumerate(o_refs):
        for t in range(CHUNK):
            m, r = divmod(t, gps)
            acc = None
            for gl in range(gps):
                y = ys[slab * gps + gl][:, m * lanes:(m + 1) * lanes]
                shift = ((gl - r) * n_ch) % lanes
                y = y if shift == 0 else pltpu.roll(y, shift, axis=1)
                acc = y if acc is None else jnp.where(piece == gl, y, acc)
            o_ref[pl.ds(t, n_cb, stride=CHUNK), :] = acc


def _s5_state_kernel(*refs, n_slabs, n_ch):
    u_refs, (w_ref, uf_ref, sa_ref, sb_ref) = refs[:n_slabs], refs[n_slabs:]
    flat = _chunk_rows_to_lanes(u_refs, uf_ref.shape[1], n_ch)
    n_groups, n_cb, _ = uf_ref.shape
    half = sa_ref.shape[1]
    for g, f in enumerate(flat):
        f = f.astype(jnp.bfloat16)
        uf_ref[g] = f
        s = jnp.dot(f, w_ref[g], preferred_element_type=jnp.float32)
        sa_ref[pl.ds(g, n_cb, stride=n_groups), :] = s[:, :half]
        sb_ref[pl.ds(g, n_cb, stride=n_groups), :] = s[:, half:]


def _s5_scan_kernel(sa_ref, sb_ref, a1_ref, a2_ref, a2s_ref, x0_ref, x_ref, xs_ref):
    @pl.when(pl.program_id(0) == 0)
    def _():
        x_ref[...] = jnp.zeros(x_ref.shape, jnp.float32)
        xs_ref[...] = jnp.zeros(xs_ref.shape, jnp.float32)

    a1, a2, a2s = a1_ref[...], a2_ref[...], a2s_ref[...]

    def body(c, carry):
        x, xs = carry
        x0_ref[c] = x
        return a1 * x + a2 * xs + sa_ref[c], a1 * xs + a2s * x + sb_ref[c]

    x, xs = lax.fori_loop(0, sa_ref.shape[0], body, (x_ref[...], xs_ref[...]), unroll=4)
    x_ref[...] = x
    xs_ref[...] = xs


def _s5_out_kernel(*refs, n_slabs, n_ch):
    u_refs = refs[:n_slabs]
    uf_ref, x0_ref, t_ref, e_ref, d_ref, wglu_ref, bglu_ref, o_ref = refs[n_slabs:n_slabs + 8]
    y_refs = refs[n_slabs + 8:]
    n_groups, n_cb, _ = uf_ref.shape
    ys = [jnp.dot(uf_ref[g], t_ref[g], preferred_element_type=jnp.float32)
          + jnp.dot(x0_ref[pl.ds(g, n_cb, stride=n_groups), :].astype(jnp.bfloat16), e_ref[g],
                    preferred_element_type=jnp.float32)
          for g in range(n_groups)]
    _chunk_lanes_to_rows(ys, y_refs, uf_ref.shape[1], n_ch)
    y = jnp.concatenate([y_ref[...] for y_ref in y_refs], axis=1)
    u = jnp.concatenate([u_ref[...] for u_ref in u_refs], axis=1)
    y = jax.nn.gelu(y + d_ref[...] * u)
    gate = jnp.dot(y.astype(jnp.bfloat16), wglu_ref[...],
                   preferred_element_type=jnp.float32) + bglu_ref[...]
    o_ref[...] = (y * jax.nn.sigmoid(gate)).astype(o_ref.dtype)


def _s5_mixer(u, tmat, wmat, emat, a1, a2, a2s, d, wglu, bglu):
    s_len, width = u.shape
    n_groups, cw, _ = tmat.shape
    n_ch = cw // CHUNK
    n_state = a1.shape[1] // 2
    n_chunks = s_len // CHUNK
    rows = S5_ROWS
    n_cb = rows // CHUNK
    lanes = V7X_LANES
    n_slabs = width // lanes
    params = pltpu.CompilerParams(dimension_semantics=("arbitrary",), vmem_limit_bytes=V7X_VMEM_LIMIT)
    slab_specs = [pl.BlockSpec((rows, lanes), functools.partial(lambda k, i: (i, k), k))
                  for k in range(n_slabs)]
    const = lambda a: pl.BlockSpec(a.shape, lambda i: (0,) * a.ndim, pipeline_mode=pl.Buffered(1))
    blk = lambda w: pl.BlockSpec((n_groups, n_cb, w), lambda i: (0, i, 0))
    cg_rows = pl.BlockSpec((n_cb * n_groups, 2 * n_state), lambda i: (i, 0))
    cg_shape = jax.ShapeDtypeStruct((n_chunks * n_groups, 2 * n_state), jnp.float32)
    u_flat, s_a, s_b = pl.pallas_call(
        functools.partial(_s5_state_kernel, n_slabs=n_slabs, n_ch=n_ch),
        grid=(s_len // rows,),
        in_specs=slab_specs + [const(wmat)],
        out_specs=[blk(cw), cg_rows, cg_rows],
        out_shape=[jax.ShapeDtypeStruct((n_groups, n_chunks, cw), jnp.bfloat16), cg_shape, cg_shape],
        compiler_params=params, name="s5_state",
    )(*([u] * n_slabs), wmat)
    cb = SCAN_BLOCK
    per_chunk = pl.BlockSpec((cb, n_groups, 2 * n_state), lambda i: (i, 0, 0))
    as_chunks = lambda a: a.reshape(n_chunks, n_groups, 2 * n_state)
    x0 = pl.pallas_call(
        _s5_scan_kernel,
        grid=(n_chunks // cb,),
        in_specs=[per_chunk, per_chunk,
                  pl.BlockSpec(a1.shape, lambda i: (0, 0)),
                  pl.BlockSpec(a2.shape, lambda i: (0, 0)),
                  pl.BlockSpec(a2s.shape, lambda i: (0, 0))],
        out_specs=per_chunk,
        out_shape=jax.ShapeDtypeStruct((n_chunks, n_groups, 2 * n_state), jnp.float32),
        scratch_shapes=[pltpu.VMEM(a1.shape, jnp.float32), pltpu.VMEM(a1.shape, jnp.float32)],
        compiler_params=params, name="s5_scan",
    )(as_chunks(s_a), as_chunks(s_b), a1, a2, a2s)
    return pl.pallas_call(
        functools.partial(_s5_out_kernel, n_slabs=n_slabs, n_ch=n_ch),
        grid=(s_len // rows,),
        in_specs=slab_specs + [blk(cw), cg_rows, const(tmat), const(emat),
                               const(d), const(wglu), const(bglu)],
        out_specs=pl.BlockSpec((rows, width), lambda i: (i, 0)),
        out_shape=jax.ShapeDtypeStruct((s_len, width), jnp.bfloat16),
        scratch_shapes=[pltpu.VMEM((rows, lanes), jnp.float32) for _ in range(n_slabs)],
        compiler_params=params, name="s5_out",
    )(*([u] * n_slabs), u_flat, x0.reshape(cg_shape.shape), tmat, emat, d, wglu, bglu)


def _mix_ffn_kernel(x_ref, attn_ref, ssm_ref, woa_ref, wos_ref, g2_ref, wg_ref, wu_ref, wd_ref, gf_ref,
                    o_ref, act_ref, *, tf):
    mixed = (jnp.dot(attn_ref[...], woa_ref[...], preferred_element_type=jnp.float32)
             + jnp.dot(ssm_ref[...], wos_ref[...], preferred_element_type=jnp.float32))
    h = x_ref[...] + mixed
    hn = _rms(h, g2_ref[...]).astype(jnp.bfloat16)
    for f in range(0, wg_ref.shape[1], tf):
        gate = jnp.dot(hn, wg_ref[:, f:f + tf], preferred_element_type=jnp.float32)
        up = jnp.dot(hn, wu_ref[:, f:f + tf], preferred_element_type=jnp.float32)
        act_ref[:, f:f + tf] = (jax.nn.silu(gate) * up).astype(jnp.bfloat16)
    ffn = jnp.dot(act_ref[...], wd_ref[...], preferred_element_type=jnp.float32)
    o_ref[...] = _rms(h + ffn, gf_ref[...])


def _mix_ffn(x, attn, ssm, wo_attn, wo_ssm, g2, wg, wu, wd, gf):
    s_len, dm = x.shape
    bm = FFN_ROWS
    d_ff = wg.shape[1]
    row = lambda w: pl.BlockSpec((bm, w), lambda i: (i, 0))
    const = lambda a: pl.BlockSpec(a.shape, lambda i: (0, 0), pipeline_mode=pl.Buffered(1))
    return pl.pallas_call(
        functools.partial(_mix_ffn_kernel, tf=FFN_TILE),
        grid=(s_len // bm,),
        in_specs=[row(dm), row(attn.shape[1]), row(ssm.shape[1]),
                  const(wo_attn), const(wo_ssm), const(g2), const(wg), const(wu), const(wd), const(gf)],
        out_specs=row(dm),
        out_shape=jax.ShapeDtypeStruct((s_len, dm), jnp.float32),
        scratch_shapes=[pltpu.VMEM((bm, d_ff), jnp.bfloat16)],
        compiler_params=pltpu.CompilerParams(dimension_semantics=("arbitrary",),
                                             vmem_limit_bytes=V7X_VMEM_LIMIT),
        name="mix_ffn",
    )(x, attn, ssm, wo_attn, wo_ssm, g2, wg, wu, wd, gf)


def kernel(x, positions, norm1_g, w_in, lambda_q1, lambda_k1, lambda_q2, lambda_k2, subln_g, ssm_lambda_re, ssm_lambda_im, ssm_log_step, ssm_b_re, ssm_b_im, ssm_c_re, ssm_c_im, ssm_d, ssm_w_glu, ssm_b_glu, w_out, norm2_g, w_gate, w_up, w_down, final_g):
    bsz, s_len, d_model = x.shape
    assert bsz == 1 and norm1_g.shape[0] == 1, "one sequence, one layer"
    assert s_len % max(IN_PROJ_ROWS, FFN_ROWS, S5_ROWS, CHUNK * SCAN_BLOCK) == 0
    f32, bf16 = jnp.float32, jnp.bfloat16
    head_dim = lambda_q1.shape[-1]
    v_dim = subln_g.shape[-1]
    rot_dim = head_dim // 4
    n_groups, n_state, n_ch = ssm_b_re.shape[1:]
    ssm_width = n_groups * n_ch
    attn_width = d_model - ssm_width
    n_heads = attn_width // v_dim
    qk_width = n_heads * 2 * head_dim
    assert w_in.shape[-1] == 2 * qk_width + attn_width + ssm_width
    assert w_gate.shape[-1] % FFN_TILE == 0

    inv_freq = ROPE_THETA ** (-jnp.arange(0, rot_dim, 2, dtype=f32) / rot_dim)
    invf = jnp.concatenate([inv_freq, inv_freq]).reshape(rot_dim, 1)
    w = w_in[0]
    n_qkv = 2 * qk_width + attn_width
    qT, k, vT, u = _in_proj(x[0], positions[0], norm1_g[0].reshape(1, d_model).astype(f32), invf,
                            w[:, :n_qkv].T.astype(bf16), w[:, n_qkv:].astype(bf16),
                            n_heads=n_heads, head_dim=head_dim, rot_dim=rot_dim, v_dim=v_dim)

    lam_vec = lambda a: a[0].reshape(1, head_dim).astype(f32)
    attn = _attention(lam_vec(lambda_q1), lam_vec(lambda_k1), lam_vec(lambda_q2), lam_vec(lambda_k2),
                      subln_g[0].astype(f32), qT, k, vT, head_dim=head_dim)

    mats = _s5_matrices(ssm_lambda_re[0], ssm_lambda_im[0], ssm_log_step[0], ssm_b_re[0], ssm_b_im[0],
                        ssm_c_re[0], ssm_c_im[0])
    ssm = _s5_mixer(u, *mats, ssm_d[0].reshape(1, ssm_width).astype(f32), ssm_w_glu[0].astype(bf16),
                    ssm_b_glu[0].reshape(1, ssm_width).astype(f32))

    wo = w_out[0].astype(bf16)
    out = _mix_ffn(x[0], attn, ssm, wo[:attn_width], wo[attn_width:],
                   norm2_g[0].reshape(1, d_model).astype(f32),
                   w_gate[0].astype(bf16), w_up[0].astype(bf16), w_down[0].astype(bf16),
                   final_g.reshape(1, d_model).astype(f32))
    return out[None]
```

```python
import functools
import math

import jax
import jax.numpy as jnp
from jax import lax
from jax.experimental import pallas as pl
from jax.experimental.pallas import tpu as pltpu

NORM_EPS = 1e-5
ROPE_THETA = 500000.0
LAMBDA_INIT = 0.8 - 0.6 * math.exp(-0.3 * 0)
LAM_RE_MAX = -1e-4
LOG2_E = math.log2(math.e)
V_EXT_ROWS = 16
CHUNK = 16
V7X_VMEM_LIMIT = 56 * 1024 * 1024
V7X_LANES = 128

IN_PROJ_ROWS = 512
ATTN_BQ = 256
ATTN_HEADS_PER_STEP = 2
ATTN_TICKS_PER_LOOP = 4
ATTN_ROW_CHUNK = 64
ATTN_MAX_RISE = 100.0
S5_ROWS = 2048
SCAN_BLOCK = 128
FFN_ROWS = 512
FFN_TILE = 256

_NT = (((1,), (1,)), ((), ()))


def _rms(x, g):
    return x * lax.rsqrt(jnp.mean(x * x, axis=-1, keepdims=True) + NORM_EPS) * g


def _in_proj_kernel(x_ref, pos_ref, g_ref, invf_ref, wT_ref, wu_ref,
                    qT_ref, k_ref, vT_ref, u_ref, *, n_maps, head_dim, rot_dim, qk_width):
    hn = _rms(x_ref[...], g_ref[...]).astype(jnp.bfloat16)
    pT = lax.dot_general(wT_ref[...], hn, _NT, preferred_element_type=jnp.float32)
    u_ref[...] = jnp.dot(hn, wu_ref[...], preferred_element_type=jnp.float32)

    ang = invf_ref[...] * pos_ref[0].astype(jnp.float32)
    cos, sin = jnp.cos(ang), jnp.sin(ang)
    half = rot_dim // 2
    c0, c1, s0, s1 = cos[:half], cos[half:], sin[:half], sin[half:]

    def rope(t, scale):
        rows = []
        for m in range(n_maps):
            r = m * head_dim
            x0, x1 = t[r:r + half], t[r + half:r + rot_dim]
            rows += [x0 * c0 - x1 * s0, x1 * c1 + x0 * s1, t[r + rot_dim:r + head_dim]]
        out = jnp.concatenate(rows, axis=0)
        return out * scale if scale != 1.0 else out

    qT_ref[...] = rope(pT[:qk_width], head_dim ** -0.5 * LOG2_E).astype(jnp.bfloat16)
    n_heads, _, v_rows, bm = vT_ref.shape
    kT = rope(pT[qk_width:2 * qk_width], 1.0)
    for h in range(n_heads):
        k_ref[h] = kT[h * 2 * head_dim:(h + 1) * 2 * head_dim].T.astype(jnp.bfloat16)
    vT = pT[2 * qk_width:].astype(jnp.bfloat16).reshape(n_heads, v_rows - V_EXT_ROWS, bm)
    ext_row = lax.broadcasted_iota(jnp.int32, (n_heads, V_EXT_ROWS, bm), 1)
    ext = jnp.where(ext_row == 0, 1.0, 0.0).astype(jnp.bfloat16)
    vT_ref[...] = jnp.concatenate([vT, ext], axis=1).reshape(vT_ref.shape)


def _in_proj(x, pos, g, invf, wT, wu, *, n_heads, head_dim, rot_dim, v_dim):
    s_len, d = x.shape
    bm = IN_PROJ_ROWS
    qk_width = n_heads * 2 * head_dim
    v_width = n_heads * v_dim
    kern = functools.partial(_in_proj_kernel, n_maps=2 * n_heads, head_dim=head_dim,
                             rot_dim=rot_dim, qk_width=qk_width)
    const = lambda i: (0, 0)
    return pl.pallas_call(
        kern,
        grid=(s_len // bm,),
        in_specs=[pl.BlockSpec((bm, d), lambda i: (i, 0)),
                  pl.BlockSpec((1, 1, bm), lambda i: (i, 0, 0)),
                  pl.BlockSpec((1, d), const),
                  pl.BlockSpec((rot_dim, 1), const),
                  pl.BlockSpec(wT.shape, const),
                  pl.BlockSpec(wu.shape, const)],
        out_specs=[pl.BlockSpec((qk_width, bm), lambda i: (0, i)),
                   pl.BlockSpec((n_heads, bm, 2 * head_dim), lambda i: (0, i, 0)),
                   pl.BlockSpec((n_heads, 1, v_dim + V_EXT_ROWS, bm), lambda i: (0, i, 0, 0)),
                   pl.BlockSpec((bm, wu.shape[1]), lambda i: (i, 0))],
        out_shape=[jax.ShapeDtypeStruct((qk_width, s_len), jnp.bfloat16),
                   jax.ShapeDtypeStruct((n_heads, s_len, 2 * head_dim), jnp.bfloat16),
                   jax.ShapeDtypeStruct((n_heads, s_len // bm, v_dim + V_EXT_ROWS, bm), jnp.bfloat16),
                   jax.ShapeDtypeStruct((s_len, wu.shape[1]), jnp.float32)],
        compiler_params=pltpu.CompilerParams(dimension_semantics=("arbitrary",),
                                             vmem_limit_bytes=V7X_VMEM_LIMIT),
        name="in_proj",
    )(x, pos.reshape(s_len // bm, 1, bm), g, invf, wT, wu)


def _attn_kernel(lq1_ref, lk1_ref, lq2_ref, lk2_ref, g_ref, qT_ref, k_ref, vT_ref, o_ref, *scratch,
                 bq, bk, head_dim, v_dim, n_par):
    scratch = list(scratch)
    take = lambda n: [scratch.pop(0) for _ in range(n)]
    qbd_ref = take(n_par)
    p_ref = [take(n_par) for _ in range(2)]
    m_ref, acc_ref, rise_ref = take(n_par), take(n_par), take(n_par)

    i = pl.program_id(1)
    kw = 2 * head_dim
    n_full = (i * bq) // bk

    zero = jnp.zeros((head_dim, bq), qT_ref.dtype)
    for h in range(n_par):
        qT = qT_ref[h * kw:(h + 1) * kw, :]
        qbd_ref[h][...] = jnp.concatenate([jnp.concatenate([qT[:head_dim], zero], axis=1),
                                           jnp.concatenate([zero, qT[head_dim:]], axis=1)], axis=0)

    def scores(t, h, masked):
        kb = k_ref[h, pl.ds(pl.multiple_of(t * bk, bk), bk), :]
        s = jnp.dot(kb, qbd_ref[h][...], preferred_element_type=jnp.float32)
        if masked:
            kpos = t * bk + lax.broadcasted_iota(jnp.int32, s.shape, 0)
            qcol = lax.broadcasted_iota(jnp.int32, s.shape, 1)
            qpos = i * bq + jnp.where(qcol >= bq, qcol - bq, qcol)
            s = jnp.where(kpos <= qpos, s, -jnp.inf)
        return s

    def pv(t, h, slot):
        return jnp.dot(vT_ref[h, t], p_ref[slot][h][...], preferred_element_type=jnp.float32)

    def max_first_step(t, h, first):
        s = scores(t, h, masked=True)
        m_blk = jnp.max(s, axis=0, keepdims=True)
        m_new = m_blk if first else jnp.maximum(m_ref[h][...], m_blk)
        p_ref[0][h][...] = jnp.exp2(s - m_new).astype(p_ref[0][h].dtype)
        if first:
            acc_ref[h][...] = pv(t, h, 0)
        else:
            acc_ref[h][...] = jnp.exp2(m_ref[h][...] - m_new) * acc_ref[h][...] + pv(t, h, 0)
        m_ref[h][...] = m_new

    def stream_step(t, slot, masked):
        for h in range(n_par):
            r = m_ref[h][...]
            s = scores(t, h, masked)
            m8 = None
            for r0 in range(0, bk, ATTN_ROW_CHUNK):
                c = s[r0:r0 + ATTN_ROW_CHUNK]
                p_ref[slot][h][r0:r0 + ATTN_ROW_CHUNK, :] = jnp.exp2(c - r).astype(p_ref[slot][h].dtype)
                c8 = jnp.max(c.reshape(ATTN_ROW_CHUNK // 8, 8, c.shape[1]), axis=0)
                m8 = c8 if m8 is None else jnp.maximum(m8, c8)
            m_blk = jnp.max(m8, axis=0, keepdims=True)
            m_new = jnp.maximum(r, m_blk)
            rise_ref[h][...] = jnp.maximum(rise_ref[h][...], m_blk - r)
            m_ref[h][...] = m_new
            acc_ref[h][...] = jnp.exp2(r - m_new) * (acc_ref[h][...] + pv(t, h, slot))

    def finish():
        lam = (jnp.exp(jnp.sum(lq1_ref[...] * lk1_ref[...], axis=-1, keepdims=True))
               - jnp.exp(jnp.sum(lq2_ref[...] * lk2_ref[...], axis=-1, keepdims=True))
               + LAMBDA_INIT)
        for h in range(n_par):
            acc = acc_ref[h][...]
            o = acc[:v_dim] / acc[v_dim:v_dim + 1]
            oT = o[:, :bq] - lam * o[:, bq:]
            y = oT * lax.rsqrt(jnp.mean(oT * oT, axis=0, keepdims=True) + NORM_EPS) * g_ref[...]
            o_ref[:, h * v_dim:(h + 1) * v_dim] = (y * (1.0 - LAMBDA_INIT)).T.astype(o_ref.dtype)

    for h in range(n_par):
        max_first_step(0, h, first=True)
        rise_ref[h][...] = jnp.zeros(rise_ref[h].shape, jnp.float32)

    n_stream = jnp.maximum(n_full - 1, 0)

    def group(c, carry):
        t = 1 + ATTN_TICKS_PER_LOOP * c
        for r in range(ATTN_TICKS_PER_LOOP):
            stream_step(t + r, r % 2, masked=False)
        return carry

    lax.fori_loop(0, n_stream // ATTN_TICKS_PER_LOOP, group, 0)

    def single(t, carry):
        stream_step(t, 0, masked=False)
        return carry

    lax.fori_loop(1 + (n_stream // ATTN_TICKS_PER_LOOP) * ATTN_TICKS_PER_LOOP, n_full, single, 0)

    @pl.when(n_full >= 1)
    def _():
        stream_step(n_full, 1, masked=True)

    finish()

    worst = rise_ref[0][...]
    for h in range(1, n_par):
        worst = jnp.maximum(worst, rise_ref[h][...])

    @pl.when(jnp.max(worst) > ATTN_MAX_RISE)
    def _():
        def redo(t, carry):
            for h in range(n_par):
                max_first_step(t, h, first=False)
            return carry

        for h in range(n_par):
            max_first_step(0, h, first=True)
        lax.fori_loop(1, n_full + 1, redo, 0)
        finish()


def _attention(lq1, lk1, lq2, lk2, subln_g, qT, k, vT, *, head_dim):
    n_heads, n_kblk, v_rows, bk = vT.shape
    v_dim = v_rows - V_EXT_ROWS
    s_len = k.shape[1]
    bq, n_par = ATTN_BQ, ATTN_HEADS_PER_STEP
    kern = functools.partial(_attn_kernel, bq=bq, bk=bk, head_dim=head_dim, v_dim=v_dim, n_par=n_par)
    vec = pl.BlockSpec((1, head_dim), lambda h, i: (0, 0))
    f32, bf16 = jnp.float32, jnp.bfloat16
    per_head = lambda shape, dtype, n=1: [pltpu.VMEM(shape, dtype) for _ in range(n * n_par)]
    scratch = (per_head((2 * head_dim, 2 * bq), bf16)
               + per_head((bk, 2 * bq), bf16, 2)
               + per_head((1, 2 * bq), f32)
               + per_head((v_rows, 2 * bq), f32)
               + per_head((1, 2 * bq), f32))
    return pl.pallas_call(
        kern,
        grid=(n_heads // n_par, s_len // bq),
        in_specs=[vec, vec, vec, vec,
                  pl.BlockSpec((v_dim, 1), lambda h, i: (0, 0)),
                  pl.BlockSpec((n_par * 2 * head_dim, bq), lambda h, i: (h, i)),
                  pl.BlockSpec((n_par, s_len, 2 * head_dim), lambda h, i: (h, 0, 0)),
                  pl.BlockSpec((n_par, n_kblk, v_rows, bk), lambda h, i: (h, 0, 0, 0))],
        out_specs=pl.BlockSpec((bq, n_par * v_dim), lambda h, i: (i, h)),
        out_shape=jax.ShapeDtypeStruct((s_len, n_heads * v_dim), bf16),
        scratch_shapes=scratch,
        compiler_params=pltpu.CompilerParams(dimension_semantics=("arbitrary", "arbitrary"),
                                             vmem_limit_bytes=V7X_VMEM_LIMIT),
        name="attn",
    )(lq1, lk1, lq2, lk2, subln_g.reshape(v_dim, 1), qT, k, vT)


def _s5_matrices(lam_re, lam_im, log_step, b_re, b_im, c_re, c_im):
    f32 = jnp.float32
    n_groups, n_state, n_ch = b_re.shape
    lr = jnp.minimum(lam_re.astype(f32), LAM_RE_MAX)
    li = lam_im.astype(f32)
    step = jnp.exp(log_step.astype(f32))[:, None]
    mag = jnp.exp(lr * step)
    lb_re, lb_im = mag * jnp.cos(li * step), mag * jnp.sin(li * step)
    denom = lr * lr + li * li
    nr, ni = lb_re - 1.0, lb_im
    coef_re = (nr * lr + ni * li) / denom
    coef_im = (ni * lr - nr * li) / denom
    br, bi = b_re.astype(f32), b_im.astype(f32)
    bb_re = coef_re[..., None] * br - coef_im[..., None] * bi
    bb_im = coef_re[..., None] * bi + coef_im[..., None] * br
    tau = jnp.arange(CHUNK + 1, dtype=f32)[None, :, None]
    pmag = jnp.exp(lr[:, None, :] * step[:, None, :] * tau)
    pw_re = pmag * jnp.cos(li[:, None, :] * step[:, None, :] * tau)
    pw_im = pmag * jnp.sin(li[:, None, :] * step[:, None, :] * tau)
    cr, ci = c_re.astype(f32)[:, None], c_im.astype(f32)[:, None]
    cp_re = cr * pw_re[:, :, None, :] - ci * pw_im[:, :, None, :]
    cp_im = cr * pw_im[:, :, None, :] + ci * pw_re[:, :, None, :]
    kmat = jnp.sum(cp_re[..., None] * bb_re[:, None, None] - cp_im[..., None] * bb_im[:, None, None], axis=3)
    ktaps = kmat[:, :CHUNK].transpose(0, 3, 1, 2).reshape(n_groups, n_ch, CHUNK * n_ch)
    rev = CHUNK - 1 - jnp.arange(CHUNK)
    ar, ai = pw_re[:, rev][:, :, None, :], pw_im[:, rev][:, :, None, :]
    brt, bit = bb_re.transpose(0, 2, 1)[:, None], bb_im.transpose(0, 2, 1)[:, None]
    w_re = (ar * brt - ai * bit).reshape(n_groups, CHUNK * n_ch, n_state)
    w_im = (ar * bit + ai * brt).reshape(n_groups, CHUNK * n_ch, n_state)
    wmat = jnp.concatenate([w_re, w_im, w_im, w_re], axis=-1)
    e_re = cp_re[:, 1:].transpose(0, 3, 1, 2).reshape(n_groups, n_state, CHUNK * n_ch)
    e_im = cp_im[:, 1:].transpose(0, 3, 1, 2).reshape(n_groups, n_state, CHUNK * n_ch)
    emat = jnp.concatenate([e_re, -e_im], axis=1)
    al_re, al_im = pw_re[:, CHUNK], pw_im[:, CHUNK]
    a1 = jnp.concatenate([al_re, al_re], axis=-1)
    a2 = jnp.concatenate([-al_im, al_im], axis=-1)
    bf = jnp.bfloat16
    return ktaps, wmat.astype(bf), emat.astype(bf), a1, a2, -a2


def _chunk_rows_to_lanes(u_refs, n_cb, n_ch):
    lanes = u_refs[0].shape[1]
    gps = lanes // n_ch
    piece = lax.broadcasted_iota(jnp.int32, (n_cb, lanes), 1) // n_ch
    out = []
    for u_ref in u_refs:
        rows = [u_ref[pl.ds(j, n_cb, stride=CHUNK), :] for j in range(CHUNK)]
        for gl in range(gps):
            cols = []
            for m in range(CHUNK // gps):
                acc = None
                for r in range(gps):
                    x = rows[gps * m + r]
                    shift = ((r - gl) * n_ch) % lanes
                    x = x if shift == 0 else pltpu.roll(x, shift, axis=1)
                    acc = x if acc is None else jnp.where(piece == r, x, acc)
                cols.append(acc)
            out.append(jnp.concatenate(cols, axis=1))
    return out


def _chunk_lanes_to_rows(ys, o_refs, n_cb, n_ch):
    lanes = o_refs[0].shape[1]
    gps = lanes // n_ch
    piece = lax.broadcasted_iota(jnp.int32, (n_cb, lanes), 1) // n_ch
    for slab, o_ref in enumerate(o_refs):
        for t in range(CHUNK):
            m, r = divmod(t, gps)
            acc = None
            for gl in range(gps):
                y = ys[slab * gps + gl][:, m * lanes:(m + 1) * lanes]
                shift = ((gl - r) * n_ch) % lanes
                y = y if shift == 0 else pltpu.roll(y, shift, axis=1)
                acc = y if acc is None else jnp.where(piece == gl, y, acc)
            o_ref[pl.ds(t, n_cb, stride=CHUNK), :] = acc


def _s5_state_kernel(*refs, n_slabs, n_ch):
    u_refs, (w_ref, uf_ref, sa_ref, sb_ref) = refs[:n_slabs], refs[n_slabs:]
    flat = _chunk_rows_to_lanes(u_refs, uf_ref.shape[1], n_ch)
    n_groups, n_cb, _ = uf_ref.shape
    half = sa_ref.shape[1]
    for g, f in enumerate(flat):
        f = f.astype(jnp.bfloat16)
        uf_ref[g] = f
        s = jnp.dot(f, w_ref[g], preferred_element_type=jnp.float32)
        sa_ref[pl.ds(g, n_cb, stride=n_groups), :] = s[:, :half]
        sb_ref[pl.ds(g, n_cb, stride=n_groups), :] = s[:, half:]


def _s5_scan_kernel(sa_ref, sb_ref, a1_ref, a2_ref, a2s_ref, x0_ref, x_ref, xs_ref):
    @pl.when(pl.program_id(0) == 0)
    def _():
        x_ref[...] = jnp.zeros(x_ref.shape, jnp.float32)
        xs_ref[...] = jnp.zeros(xs_ref.shape, jnp.float32)

    a1, a2, a2s = a1_ref[...], a2_ref[...], a2s_ref[...]

    def body(c, carry):
        x, xs = carry
        x0_ref[c] = x
        return a1 * x + a2 * xs + sa_ref[c], a1 * xs + a2s * x + sb_ref[c]

    x, xs = lax.fori_loop(0, sa_ref.shape[0], body, (x_ref[...], xs_ref[...]), unroll=4)
    x_ref[...] = x
    xs_ref[...] = xs


def _s5_out_kernel(*refs, n_slabs, n_ch):
    u_refs = refs[:n_slabs]
    uf_ref, x0_ref, kt_ref, e_ref, d_ref, wglu_ref, bglu_ref, o_ref, t_ref = refs[n_slabs:n_slabs + 9]
    y_refs = refs[n_slabs + 9:]
    n_groups, n_cb, cw = uf_ref.shape

    @pl.when(pl.program_id(0) == 0)
    def _():
        lane = lax.broadcasted_iota(jnp.int32, (n_ch, cw), 1)
        for g in range(n_groups):
            taps = kt_ref[g]
            for j in range(CHUNK):
                rows = taps if j == 0 else jnp.where(lane >= j * n_ch,
                                                     pltpu.roll(taps, j * n_ch, axis=1), 0.0)
                t_ref[g, j * n_ch:(j + 1) * n_ch, :] = rows.astype(t_ref.dtype)

    ys = [jnp.dot(uf_ref[g], t_ref[g], preferred_element_type=jnp.float32)
          + jnp.dot(x0_ref[pl.ds(g, n_cb, stride=n_groups), :].astype(jnp.bfloat16), e_ref[g],
                    preferred_element_type=jnp.float32)
          for g in range(n_groups)]
    _chunk_lanes_to_rows(ys, y_refs, uf_ref.shape[1], n_ch)
    y = jnp.concatenate([y_ref[...] for y_ref in y_refs], axis=1)
    u = jnp.concatenate([u_ref[...] for u_ref in u_refs], axis=1)
    y = jax.nn.gelu(y + d_ref[...] * u)
    gate = jnp.dot(y.astype(jnp.bfloat16), wglu_ref[...],
                   preferred_element_type=jnp.float32) + bglu_ref[...]
    o_ref[...] = (y * jax.nn.sigmoid(gate)).astype(o_ref.dtype)


def _s5_mixer(u, ktaps, wmat, emat, a1, a2, a2s, d, wglu, bglu):
    s_len, width = u.shape
    n_groups, n_ch, cw = ktaps.shape
    n_state = a1.shape[1] // 2
    n_chunks = s_len // CHUNK
    rows = S5_ROWS
    n_cb = rows // CHUNK
    lanes = V7X_LANES
    n_slabs = width // lanes
    params = pltpu.CompilerParams(dimension_semantics=("arbitrary",), vmem_limit_bytes=V7X_VMEM_LIMIT)
    slab_specs = [pl.BlockSpec((rows, lanes), functools.partial(lambda k, i: (i, k), k))
                  for k in range(n_slabs)]
    const = lambda a: pl.BlockSpec(a.shape, lambda i: (0,) * a.ndim, pipeline_mode=pl.Buffered(1))
    blk = lambda w: pl.BlockSpec((n_groups, n_cb, w), lambda i: (0, i, 0))
    cg_rows = pl.BlockSpec((n_cb * n_groups, 2 * n_state), lambda i: (i, 0))
    cg_shape = jax.ShapeDtypeStruct((n_chunks * n_groups, 2 * n_state), jnp.float32)
    u_flat, s_a, s_b = pl.pallas_call(
        functools.partial(_s5_state_kernel, n_slabs=n_slabs, n_ch=n_ch),
        grid=(s_len // rows,),
        in_specs=slab_specs + [const(wmat)],
        out_specs=[blk(cw), cg_rows, cg_rows],
        out_shape=[jax.ShapeDtypeStruct((n_groups, n_chunks, cw), jnp.bfloat16), cg_shape, cg_shape],
        compiler_params=params, name="s5_state",
    )(*([u] * n_slabs), wmat)
    cb = SCAN_BLOCK
    per_chunk = pl.BlockSpec((cb, n_groups, 2 * n_state), lambda i: (i, 0, 0))
    as_chunks = lambda a: a.reshape(n_chunks, n_groups, 2 * n_state)
    x0 = pl.pallas_call(
        _s5_scan_kernel,
        grid=(n_chunks // cb,),
        in_specs=[per_chunk, per_chunk,
                  pl.BlockSpec(a1.shape, lambda i: (0, 0)),
                  pl.BlockSpec(a2.shape, lambda i: (0, 0)),
                  pl.BlockSpec(a2s.shape, lambda i: (0, 0))],
        out_specs=per_chunk,
        out_shape=jax.ShapeDtypeStruct((n_chunks, n_groups, 2 * n_state), jnp.float32),
        scratch_shapes=[pltpu.VMEM(a1.shape, jnp.float32), pltpu.VMEM(a1.shape, jnp.float32)],
        compiler_params=params, name="s5_scan",
    )(as_chunks(s_a), as_chunks(s_b), a1, a2, a2s)
    return pl.pallas_call(
        functools.partial(_s5_out_kernel, n_slabs=n_slabs, n_ch=n_ch),
        grid=(s_len // rows,),
        in_specs=slab_specs + [blk(cw), cg_rows, const(ktaps), const(emat),
                               const(d), const(wglu), const(bglu)],
        out_specs=pl.BlockSpec((rows, width), lambda i: (i, 0)),
        out_shape=jax.ShapeDtypeStruct((s_len, width), jnp.bfloat16),
        scratch_shapes=([pltpu.VMEM((n_groups, cw, cw), jnp.bfloat16)]
                        + [pltpu.VMEM((rows, lanes), jnp.float32) for _ in range(n_slabs)]),
        compiler_params=params, name="s5_out",
    )(*([u] * n_slabs), u_flat, x0.reshape(cg_shape.shape), ktaps, emat, d, wglu, bglu)


def _mix_ffn_kernel(x_ref, attn_ref, ssm_ref, woa_ref, wos_ref, g2_ref, wg_ref, wu_ref, wd_ref, gf_ref,
                    o_ref, act_ref, *, tf):
    mixed = (jnp.dot(attn_ref[...], woa_ref[...], preferred_element_type=jnp.float32)
             + jnp.dot(ssm_ref[...], wos_ref[...], preferred_element_type=jnp.float32))
    h = x_ref[...] + mixed
    hn = _rms(h, g2_ref[...]).astype(jnp.bfloat16)
    for f in range(0, wg_ref.shape[1], tf):
        gate = jnp.dot(hn, wg_ref[:, f:f + tf], preferred_element_type=jnp.float32)
        up = jnp.dot(hn, wu_ref[:, f:f + tf], preferred_element_type=jnp.float32)
        act_ref[:, f:f + tf] = (jax.nn.silu(gate) * up).astype(jnp.bfloat16)
    ffn = jnp.dot(act_ref[...], wd_ref[...], preferred_element_type=jnp.float32)
    o_ref[...] = _rms(h + ffn, gf_ref[...])


def _mix_ffn(x, attn, ssm, wo_attn, wo_ssm, g2, wg, wu, wd, gf):
    s_len, dm = x.shape
    bm = FFN_ROWS
    d_ff = wg.shape[1]
    row = lambda w: pl.BlockSpec((bm, w), lambda i: (i, 0))
    const = lambda a: pl.BlockSpec(a.shape, lambda i: (0, 0), pipeline_mode=pl.Buffered(1))
    return pl.pallas_call(
        functools.partial(_mix_ffn_kernel, tf=FFN_TILE),
        grid=(s_len // bm,),
        in_specs=[row(dm), row(attn.shape[1]), row(ssm.shape[1]),
                  const(wo_attn), const(wo_ssm), const(g2), const(wg), const(wu), const(wd), const(gf)],
        out_specs=row(dm),
        out_shape=jax.ShapeDtypeStruct((s_len, dm), jnp.float32),
        scratch_shapes=[pltpu.VMEM((bm, d_ff), jnp.bfloat16)],
        compiler_params=pltpu.CompilerParams(dimension_semantics=("arbitrary",),
                                             vmem_limit_bytes=V7X_VMEM_LIMIT),
        name="mix_ffn",
    )(x, attn, ssm, wo_attn, wo_ssm, g2, wg, wu, wd, gf)


def kernel(x, positions, norm1_g, w_in, lambda_q1, lambda_k1, lambda_q2, lambda_k2, subln_g, ssm_lambda_re, ssm_lambda_im, ssm_log_step, ssm_b_re, ssm_b_im, ssm_c_re, ssm_c_im, ssm_d, ssm_w_glu, ssm_b_glu, w_out, norm2_g, w_gate, w_up, w_down, final_g):
    bsz, s_len, d_model = x.shape
    assert bsz == 1 and norm1_g.shape[0] == 1, "one sequence, one layer"
    assert s_len % max(IN_PROJ_ROWS, FFN_ROWS, S5_ROWS, CHUNK * SCAN_BLOCK) == 0
    f32, bf16 = jnp.float32, jnp.bfloat16
    head_dim = lambda_q1.shape[-1]
    v_dim = subln_g.shape[-1]
    rot_dim = head_dim // 4
    n_groups, n_state, n_ch = ssm_b_re.shape[1:]
    ssm_width = n_groups * n_ch
    attn_width = d_model - ssm_width
    n_heads = attn_width // v_dim
    qk_width = n_heads * 2 * head_dim
    assert w_in.shape[-1] == 2 * qk_width + attn_width + ssm_width
    assert w_gate.shape[-1] % FFN_TILE == 0

    inv_freq = ROPE_THETA ** (-jnp.arange(0, rot_dim, 2, dtype=f32) / rot_dim)
    invf = jnp.concatenate([inv_freq, inv_freq]).reshape(rot_dim, 1)
    w = w_in[0]
    n_qkv = 2 * qk_width + attn_width
    qT, k, vT, u = _in_proj(x[0], positions[0], norm1_g[0].reshape(1, d_model).astype(f32), invf,
                            w[:, :n_qkv].T.astype(bf16), w[:, n_qkv:].astype(bf16),
                            n_heads=n_heads, head_dim=head_dim, rot_dim=rot_dim, v_dim=v_dim)

    lam_vec = lambda a: a[0].reshape(1, head_dim).astype(f32)
    attn = _attention(lam_vec(lambda_q1), lam_vec(lambda_k1), lam_vec(lambda_q2), lam_vec(lambda_k2),
                      subln_g[0].astype(f32), qT, k, vT, head_dim=head_dim)

    mats = _s5_matrices(ssm_lambda_re[0], ssm_lambda_im[0], ssm_log_step[0], ssm_b_re[0], ssm_b_im[0],
                        ssm_c_re[0], ssm_c_im[0])
    ssm = _s5_mixer(u, *mats, ssm_d[0].reshape(1, ssm_width).astype(f32), ssm_w_glu[0].astype(bf16),
                    ssm_b_glu[0].reshape(1, ssm_width).astype(f32))

    wo = w_out[0].astype(bf16)
    out = _mix_ffn(x[0], attn, ssm, wo[:attn_width], wo[attn_width:],
                   norm2_g[0].reshape(1, d_model).astype(f32),
                   w_gate[0].astype(bf16), w_up[0].astype(bf16), w_down[0].astype(bf16),
                   final_g.reshape(1, d_model).astype(f32))
    return out[None]
```

```python
import functools
import math

import jax
import jax.numpy as jnp
from jax import lax
from jax.experimental import pallas as pl
from jax.experimental.pallas import tpu as pltpu

NORM_EPS = 1e-5
ROPE_THETA = 500000.0
LAMBDA_INIT = 0.8 - 0.6 * math.exp(-0.3 * 0)
LAM_RE_MAX = -1e-4
LOG2_E = math.log2(math.e)
V_EXT_ROWS = 16
CHUNK = 16
V7X_VMEM_LIMIT = 56 * 1024 * 1024
V7X_LANES = 128

IN_PROJ_ROWS = 512
ATTN_BQ = 256
ATTN_HEADS_PER_STEP = 4
ATTN_TICKS_PER_LOOP = 2
ATTN_ROW_CHUNK = 64
ATTN_MAX_RISE = 100.0
S5_ROWS = 2048
S5_GROUP_PITCH = 40
S5_STAGE_PAD = 8
SCAN_BLOCK = 128
FFN_ROWS = 512
FFN_TILE = 256

_NT = (((1,), (1,)), ((), ()))


def _rms(x, g):
    return x * lax.rsqrt(jnp.mean(x * x, axis=-1, keepdims=True) + NORM_EPS) * g


def _in_proj_kernel(x_ref, pos_ref, g_ref, invf_ref, wT_ref, wu_ref,
                    qT_ref, k_ref, vT_ref, u_ref, *, n_maps, head_dim, rot_dim, qk_width):
    hn = _rms(x_ref[...], g_ref[...]).astype(jnp.bfloat16)
    pT = lax.dot_general(wT_ref[...], hn, _NT, preferred_element_type=jnp.float32)
    u_ref[...] = jnp.dot(hn, wu_ref[...], preferred_element_type=jnp.float32)

    ang = invf_ref[...] * pos_ref[0].astype(jnp.float32)
    cos, sin = jnp.cos(ang), jnp.sin(ang)
    half = rot_dim // 2
    c0, c1, s0, s1 = cos[:half], cos[half:], sin[:half], sin[half:]

    def rope(t, scale):
        rows = []
        for m in range(n_maps):
            r = m * head_dim
            x0, x1 = t[r:r + half], t[r + half:r + rot_dim]
            rows += [x0 * c0 - x1 * s0, x1 * c1 + x0 * s1, t[r + rot_dim:r + head_dim]]
        out = jnp.concatenate(rows, axis=0)
        return out * scale if scale != 1.0 else out

    qT_ref[...] = rope(pT[:qk_width], head_dim ** -0.5 * LOG2_E).astype(jnp.bfloat16)
    n_heads, _, v_rows, bm = vT_ref.shape
    kT = rope(pT[qk_width:2 * qk_width], 1.0)
    for h in range(n_heads):
        k_ref[h] = kT[h * 2 * head_dim:(h + 1) * 2 * head_dim].T.astype(jnp.bfloat16)
    vT = pT[2 * qk_width:].astype(jnp.bfloat16).reshape(n_heads, v_rows - V_EXT_ROWS, bm)
    ext_row = lax.broadcasted_iota(jnp.int32, (n_heads, V_EXT_ROWS, bm), 1)
    ext = jnp.where(ext_row == 0, 1.0, 0.0).astype(jnp.bfloat16)
    vT_ref[...] = jnp.concatenate([vT, ext], axis=1).reshape(vT_ref.shape)


def _in_proj(x, pos, g, invf, wT, wu, *, n_heads, head_dim, rot_dim, v_dim):
    s_len, d = x.shape
    bm = IN_PROJ_ROWS
    qk_width = n_heads * 2 * head_dim
    v_width = n_heads * v_dim
    kern = functools.partial(_in_proj_kernel, n_maps=2 * n_heads, head_dim=head_dim,
                             rot_dim=rot_dim, qk_width=qk_width)
    const = lambda i: (0, 0)
    return pl.pallas_call(
        kern,
        grid=(s_len // bm,),
        in_specs=[pl.BlockSpec((bm, d), lambda i: (i, 0)),
                  pl.BlockSpec((1, 1, bm), lambda i: (i, 0, 0)),
                  pl.BlockSpec((1, d), const),
                  pl.BlockSpec((rot_dim, 1), const),
                  pl.BlockSpec(wT.shape, const),
                  pl.BlockSpec(wu.shape, const)],
        out_specs=[pl.BlockSpec((qk_width, bm), lambda i: (0, i)),
                   pl.BlockSpec((n_heads, bm, 2 * head_dim), lambda i: (0, i, 0)),
                   pl.BlockSpec((n_heads, 1, v_dim + V_EXT_ROWS, bm), lambda i: (0, i, 0, 0)),
                   pl.BlockSpec((bm, wu.shape[1]), lambda i: (i, 0))],
        out_shape=[jax.ShapeDtypeStruct((qk_width, s_len), jnp.bfloat16),
                   jax.ShapeDtypeStruct((n_heads, s_len, 2 * head_dim), jnp.bfloat16),
                   jax.ShapeDtypeStruct((n_heads, s_len // bm, v_dim + V_EXT_ROWS, bm), jnp.bfloat16),
                   jax.ShapeDtypeStruct((s_len, wu.shape[1]), jnp.float32)],
        compiler_params=pltpu.CompilerParams(dimension_semantics=("arbitrary",),
                                             vmem_limit_bytes=V7X_VMEM_LIMIT),
        name="in_proj",
    )(x, pos.reshape(s_len // bm, 1, bm), g, invf, wT, wu)


def _attn_kernel(lq1_ref, lk1_ref, lq2_ref, lk2_ref, g_ref, qT_ref, k_ref, vT_ref, o_ref, *scratch,
                 bq, bk, head_dim, v_dim, n_par):
    scratch = list(scratch)
    take = lambda n: [scratch.pop(0) for _ in range(n)]
    qbd_ref = take(n_par)
    p_ref = [take(n_par) for _ in range(2)]
    m_ref, acc_ref, rise_ref = take(n_par), take(n_par), take(n_par)

    i = pl.program_id(1)
    kw = 2 * head_dim
    n_full = (i * bq) // bk

    zero = jnp.zeros((head_dim, bq), qT_ref.dtype)
    for h in range(n_par):
        qT = qT_ref[h * kw:(h + 1) * kw, :]
        qbd_ref[h][...] = jnp.concatenate([jnp.concatenate([qT[:head_dim], zero], axis=1),
                                           jnp.concatenate([zero, qT[head_dim:]], axis=1)], axis=0)

    def scores(t, h, masked):
        kb = k_ref[h, pl.ds(pl.multiple_of(t * bk, bk), bk), :]
        s = jnp.dot(kb, qbd_ref[h][...], preferred_element_type=jnp.float32)
        if masked:
            kpos = t * bk + lax.broadcasted_iota(jnp.int32, s.shape, 0)
            qcol = lax.broadcasted_iota(jnp.int32, s.shape, 1)
            qpos = i * bq + jnp.where(qcol >= bq, qcol - bq, qcol)
            s = jnp.where(kpos <= qpos, s, -jnp.inf)
        return s

    def pv(t, h, slot):
        return jnp.dot(vT_ref[h, t], p_ref[slot][h][...], preferred_element_type=jnp.float32)

    def max_first_step(t, h, first):
        s = scores(t, h, masked=True)
        m_blk = jnp.max(s, axis=0, keepdims=True)
        m_new = m_blk if first else jnp.maximum(m_ref[h][...], m_blk)
        p_ref[0][h][...] = jnp.exp2(s - m_new).astype(p_ref[0][h].dtype)
        if first:
            acc_ref[h][...] = pv(t, h, 0)
        else:
            acc_ref[h][...] = jnp.exp2(m_ref[h][...] - m_new) * acc_ref[h][...] + pv(t, h, 0)
        m_ref[h][...] = m_new

    def stream_step(t, slot, masked):
        for h in range(n_par):
            r = m_ref[h][...]
            s = scores(t, h, masked)
            m8 = None
            for r0 in range(0, bk, ATTN_ROW_CHUNK):
                c = s[r0:r0 + ATTN_ROW_CHUNK]
                p_ref[slot][h][r0:r0 + ATTN_ROW_CHUNK, :] = jnp.exp2(c - r).astype(p_ref[slot][h].dtype)
                c8 = jnp.max(c.reshape(ATTN_ROW_CHUNK // 8, 8, c.shape[1]), axis=0)
                m8 = c8 if m8 is None else jnp.maximum(m8, c8)
            m_blk = jnp.max(m8, axis=0, keepdims=True)
            m_new = jnp.maximum(r, m_blk)
            rise_ref[h][...] = jnp.maximum(rise_ref[h][...], m_blk - r)
            m_ref[h][...] = m_new
            acc_ref[h][...] = jnp.exp2(r - m_new) * (acc_ref[h][...] + pv(t, h, slot))

    def finish():
        lam = (jnp.exp(jnp.sum(lq1_ref[...] * lk1_ref[...], axis=-1, keepdims=True))
               - jnp.exp(jnp.sum(lq2_ref[...] * lk2_ref[...], axis=-1, keepdims=True))
               + LAMBDA_INIT)
        for h in range(n_par):
            acc = acc_ref[h][...]
            o = acc[:v_dim] / acc[v_dim:v_dim + 1]
            oT = o[:, :bq] - lam * o[:, bq:]
            y = oT * lax.rsqrt(jnp.mean(oT * oT, axis=0, keepdims=True) + NORM_EPS) * g_ref[...]
            o_ref[:, h * v_dim:(h + 1) * v_dim] = (y * (1.0 - LAMBDA_INIT)).T.astype(o_ref.dtype)

    for h in range(n_par):
        max_first_step(0, h, first=True)
        rise_ref[h][...] = jnp.zeros(rise_ref[h].shape, jnp.float32)

    n_stream = jnp.maximum(n_full - 1, 0)

    def group(c, carry):
        t = 1 + ATTN_TICKS_PER_LOOP * c
        for r in range(ATTN_TICKS_PER_LOOP):
            stream_step(t + r, r % 2, masked=False)
        return carry

    lax.fori_loop(0, n_stream // ATTN_TICKS_PER_LOOP, group, 0)

    def single(t, carry):
        stream_step(t, 0, masked=False)
        return carry

    lax.fori_loop(1 + (n_stream // ATTN_TICKS_PER_LOOP) * ATTN_TICKS_PER_LOOP, n_full, single, 0)

    @pl.when(n_full >= 1)
    def _():
        stream_step(n_full, 1, masked=True)

    finish()

    worst = rise_ref[0][...]
    for h in range(1, n_par):
        worst = jnp.maximum(worst, rise_ref[h][...])

    @pl.when(jnp.max(worst) > ATTN_MAX_RISE)
    def _():
        def redo(t, carry):
            for h in range(n_par):
                max_first_step(t, h, first=False)
            return carry

        for h in range(n_par):
            max_first_step(0, h, first=True)
        lax.fori_loop(1, n_full + 1, redo, 0)
        finish()


def _attention(lq1, lk1, lq2, lk2, subln_g, qT, k, vT, *, head_dim):
    n_heads, n_kblk, v_rows, bk = vT.shape
    v_dim = v_rows - V_EXT_ROWS
    s_len = k.shape[1]
    bq, n_par = ATTN_BQ, ATTN_HEADS_PER_STEP
    kern = functools.partial(_attn_kernel, bq=bq, bk=bk, head_dim=head_dim, v_dim=v_dim, n_par=n_par)
    vec = pl.BlockSpec((1, head_dim), lambda h, i: (0, 0))
    f32, bf16 = jnp.float32, jnp.bfloat16
    per_head = lambda shape, dtype, n=1: [pltpu.VMEM(shape, dtype) for _ in range(n * n_par)]
    scratch = (per_head((2 * head_dim, 2 * bq), bf16)
               + per_head((bk, 2 * bq), bf16, 2)
               + per_head((1, 2 * bq), f32)
               + per_head((v_rows, 2 * bq), f32)
               + per_head((1, 2 * bq), f32))
    resident = dict(pipeline_mode=pl.Buffered(1)) if n_par == n_heads else {}
    return pl.pallas_call(
        kern,
        grid=(n_heads // n_par, s_len // bq),
        in_specs=[vec, vec, vec, vec,
                  pl.BlockSpec((v_dim, 1), lambda h, i: (0, 0)),
                  pl.BlockSpec((n_par * 2 * head_dim, bq), lambda h, i: (h, i)),
                  pl.BlockSpec((n_par, s_len, 2 * head_dim), lambda h, i: (h, 0, 0), **resident),
                  pl.BlockSpec((n_par, n_kblk, v_rows, bk), lambda h, i: (h, 0, 0, 0), **resident)],
        out_specs=pl.BlockSpec((bq, n_par * v_dim), lambda h, i: (i, h)),
        out_shape=jax.ShapeDtypeStruct((s_len, n_heads * v_dim), bf16),
        scratch_shapes=scratch,
        compiler_params=pltpu.CompilerParams(dimension_semantics=("arbitrary", "arbitrary"),
                                             vmem_limit_bytes=V7X_VMEM_LIMIT),
        name="attn",
    )(lq1, lk1, lq2, lk2, subln_g.reshape(v_dim, 1), qT, k, vT)


def _s5_matrices(lam_re, lam_im, log_step, b_re, b_im, c_re, c_im):
    f32 = jnp.float32
    n_groups, n_state, n_ch = b_re.shape
    lr = jnp.minimum(lam_re.astype(f32), LAM_RE_MAX)
    li = lam_im.astype(f32)
    step = jnp.exp(log_step.astype(f32))[:, None]
    mag = jnp.exp(lr * step)
    lb_re, lb_im = mag * jnp.cos(li * step), mag * jnp.sin(li * step)
    denom = lr * lr + li * li
    nr, ni = lb_re - 1.0, lb_im
    coef_re = (nr * lr + ni * li) / denom
    coef_im = (ni * lr - nr * li) / denom
    br, bi = b_re.astype(f32), b_im.astype(f32)
    bb_re = coef_re[..., None] * br - coef_im[..., None] * bi
    bb_im = coef_re[..., None] * bi + coef_im[..., None] * br
    tau = jnp.arange(CHUNK + 1, dtype=f32)[None, :, None]
    pmag = jnp.exp(lr[:, None, :] * step[:, None, :] * tau)
    pw_re = pmag * jnp.cos(li[:, None, :] * step[:, None, :] * tau)
    pw_im = pmag * jnp.sin(li[:, None, :] * step[:, None, :] * tau)
    cr, ci = c_re.astype(f32)[:, None], c_im.astype(f32)[:, None]
    cp_re = cr * pw_re[:, :, None, :] - ci * pw_im[:, :, None, :]
    cp_im = cr * pw_im[:, :, None, :] + ci * pw_re[:, :, None, :]
    kmat = jnp.sum(cp_re[..., None] * bb_re[:, None, None] - cp_im[..., None] * bb_im[:, None, None], axis=3)
    ktaps = kmat[:, :CHUNK].transpose(0, 3, 1, 2).reshape(n_groups, n_ch, CHUNK * n_ch)
    rev = CHUNK - 1 - jnp.arange(CHUNK)
    ar, ai = pw_re[:, rev][:, :, None, :], pw_im[:, rev][:, :, None, :]
    brt, bit = bb_re.transpose(0, 2, 1)[:, None], bb_im.transpose(0, 2, 1)[:, None]
    w_re = (ar * brt - ai * bit).reshape(n_groups, CHUNK * n_ch, n_state)
    w_im = (ar * bit + ai * brt).reshape(n_groups, CHUNK * n_ch, n_state)
    wmat = jnp.concatenate([w_re, w_im, w_im, w_re], axis=-1)
    e_re = cp_re[:, 1:].transpose(0, 3, 1, 2).reshape(n_groups, n_state, CHUNK * n_ch)
    e_im = cp_im[:, 1:].transpose(0, 3, 1, 2).reshape(n_groups, n_state, CHUNK * n_ch)
    emat = jnp.concatenate([e_re, -e_im], axis=1)
    al_re, al_im = pw_re[:, CHUNK], pw_im[:, CHUNK]
    a1 = jnp.concatenate([al_re, al_re], axis=-1)
    a2 = jnp.concatenate([-al_im, al_im], axis=-1)
    bf = jnp.bfloat16
    return ktaps, wmat.astype(bf), emat.astype(bf), a1, a2, -a2


def _transpose_pieces(arrs, width):
    n = len(arrs)
    piece = lax.broadcasted_iota(jnp.int32, arrs[0].shape, 1) // width
    arrs = list(arrs)
    d = n // 2
    while d:
        upper = (piece // d) % 2 == 1
        for a in range(n):
            if a & d == 0:
                lo, hi = arrs[a], arrs[a + d]
                arrs[a] = jnp.where(upper, pltpu.roll(hi, d * width, axis=1), lo)
                arrs[a + d] = jnp.where(upper, hi, pltpu.roll(lo, (n - d) * width, axis=1))
        d //= 2
    return arrs


def _chunk_rows_to_lanes(u_refs, n_cb, n_ch):
    gps = u_refs[0].shape[1] // n_ch
    out = []
    for u_ref in u_refs:
        rows = [u_ref[pl.ds(j, n_cb, stride=CHUNK), :] for j in range(CHUNK)]
        cols = [_transpose_pieces(rows[m * gps:(m + 1) * gps], n_ch) for m in range(CHUNK // gps)]
        out += [jnp.concatenate([c[gl] for c in cols], axis=1) for gl in range(gps)]
    return out


def _chunk_lanes_to_rows(ys, stage_refs, o_refs, n_cb, n_ch):
    lanes = o_refs[0].shape[1]
    gps = lanes // n_ch
    pitch = stage_refs[0].shape[0] // CHUNK
    for slab, stage_ref in enumerate(stage_refs):
        for m in range(CHUNK // gps):
            group_cols = [ys[slab * gps + gl][:, m * lanes:(m + 1) * lanes] for gl in range(gps)]
            for r, x in enumerate(_transpose_pieces(group_cols, n_ch)):
                t = m * gps + r
                stage_ref[t * pitch:t * pitch + n_cb, :] = x

    def chunk_rows(c, carry):
        for stage_ref, o_ref in zip(stage_refs, o_refs):
            o_ref[pl.ds(pl.multiple_of(c * CHUNK, CHUNK), CHUNK), :] = stage_ref[pl.ds(c, CHUNK, stride=pitch), :]
        return carry

    lax.fori_loop(0, n_cb, chunk_rows, 0, unroll=8)


def _s5_state_kernel(*refs, n_slabs, n_ch):
    u_refs, (w_ref, uf_ref, sa_ref, sb_ref) = refs[:n_slabs], refs[n_slabs:]
    flat = _chunk_rows_to_lanes(u_refs, uf_ref.shape[1], n_ch)
    n_groups, n_cb, _ = uf_ref.shape
    half = sa_ref.shape[1]
    for g, f in enumerate(flat):
        f = f.astype(jnp.bfloat16)
        uf_ref[g] = f
        s = jnp.dot(f, w_ref[g], preferred_element_type=jnp.float32)
        sa_ref[pl.ds(g, n_cb, stride=S5_GROUP_PITCH), :] = s[:, :half]
        sb_ref[pl.ds(g, n_cb, stride=S5_GROUP_PITCH), :] = s[:, half:]
    for g in range(n_groups, S5_GROUP_PITCH):
        sa_ref[pl.ds(g, n_cb, stride=S5_GROUP_PITCH), :] = jnp.zeros((n_cb, half), jnp.float32)
        sb_ref[pl.ds(g, n_cb, stride=S5_GROUP_PITCH), :] = jnp.zeros((n_cb, half), jnp.float32)


def _s5_scan_kernel(sa_ref, sb_ref, a1_ref, a2_ref, a2s_ref, x0_ref, x_ref, xs_ref):
    @pl.when(pl.program_id(0) == 0)
    def _():
        x_ref[...] = jnp.zeros(x_ref.shape, jnp.float32)
        xs_ref[...] = jnp.zeros(xs_ref.shape, jnp.float32)

    a1, a2, a2s = a1_ref[...], a2_ref[...], a2s_ref[...]
    n_groups = a1.shape[0]
    pad = jnp.zeros((S5_GROUP_PITCH - n_groups, a1.shape[1]), jnp.float32)

    def body(c, carry):
        x, xs = carry
        row = pl.multiple_of(c * S5_GROUP_PITCH, 8)
        x0_ref[pl.ds(row, n_groups), :] = x
        x0_ref[pl.ds(row + n_groups, S5_GROUP_PITCH - n_groups), :] = pad
        return (a1 * x + a2 * xs + sa_ref[pl.ds(row, n_groups), :],
                a1 * xs + a2s * x + sb_ref[pl.ds(row, n_groups), :])

    n_chunks = sa_ref.shape[0] // S5_GROUP_PITCH
    x, xs = lax.fori_loop(0, n_chunks, body, (x_ref[...], xs_ref[...]), unroll=4)
    x_ref[...] = x
    xs_ref[...] = xs


def _s5_out_kernel(*refs, n_slabs, n_ch):
    u_refs = refs[:n_slabs]
    uf_ref, x0_ref, kt_ref, e_ref, d_ref, wglu_ref, bglu_ref, o_ref, t_ref = refs[n_slabs:n_slabs + 9]
    stage_refs, y_refs = refs[n_slabs + 9:2 * n_slabs + 9], refs[2 * n_slabs + 9:]
    n_groups, n_cb, cw = uf_ref.shape

    @pl.when(pl.program_id(0) == 0)
    def _():
        lane = lax.broadcasted_iota(jnp.int32, (n_ch, cw), 1)
        for g in range(n_groups):
            taps = kt_ref[g]
            for j in range(CHUNK):
                rows = taps if j == 0 else jnp.where(lane >= j * n_ch,
                                                     pltpu.roll(taps, j * n_ch, axis=1), 0.0)
                t_ref[g, j * n_ch:(j + 1) * n_ch, :] = rows.astype(t_ref.dtype)

    ys = [jnp.dot(uf_ref[g], t_ref[g], preferred_element_type=jnp.float32)
          + jnp.dot(x0_ref[pl.ds(g, n_cb, stride=S5_GROUP_PITCH), :].astype(jnp.bfloat16), e_ref[g],
                    preferred_element_type=jnp.float32)
          for g in range(n_groups)]
    _chunk_lanes_to_rows(ys, stage_refs, y_refs, n_cb, n_ch)
    y = jnp.concatenate([y_ref[...] for y_ref in y_refs], axis=1)
    u = jnp.concatenate([u_ref[...] for u_ref in u_refs], axis=1)
    y = jax.nn.gelu(y + d_ref[...] * u)
    gate = jnp.dot(y.astype(jnp.bfloat16), wglu_ref[...],
                   preferred_element_type=jnp.float32) + bglu_ref[...]
    o_ref[...] = (y * jax.nn.sigmoid(gate)).astype(o_ref.dtype)


def _s5_mixer(u, ktaps, wmat, emat, a1, a2, a2s, d, wglu, bglu):
    s_len, width = u.shape
    n_groups, n_ch, cw = ktaps.shape
    n_state = a1.shape[1] // 2
    n_chunks = s_len // CHUNK
    rows = S5_ROWS
    n_cb = rows // CHUNK
    lanes = V7X_LANES
    n_slabs = width // lanes
    params = pltpu.CompilerParams(dimension_semantics=("arbitrary",), vmem_limit_bytes=V7X_VMEM_LIMIT)
    slab_specs = [pl.BlockSpec((rows, lanes), functools.partial(lambda k, i: (i, k), k))
                  for k in range(n_slabs)]
    const = lambda a: pl.BlockSpec(a.shape, lambda i: (0,) * a.ndim, pipeline_mode=pl.Buffered(1))
    blk = lambda w: pl.BlockSpec((n_groups, n_cb, w), lambda i: (0, i, 0))
    assert n_groups <= S5_GROUP_PITCH and S5_GROUP_PITCH % 8 == 0
    cg_rows = pl.BlockSpec((n_cb * S5_GROUP_PITCH, 2 * n_state), lambda i: (i, 0))
    cg_shape = jax.ShapeDtypeStruct((n_chunks * S5_GROUP_PITCH, 2 * n_state), jnp.float32)
    u_flat, s_a, s_b = pl.pallas_call(
        functools.partial(_s5_state_kernel, n_slabs=n_slabs, n_ch=n_ch),
        grid=(s_len // rows,),
        in_specs=slab_specs + [const(wmat)],
        out_specs=[blk(cw), cg_rows, cg_rows],
        out_shape=[jax.ShapeDtypeStruct((n_groups, n_chunks, cw), jnp.bfloat16), cg_shape, cg_shape],
        compiler_params=params, name="s5_state",
    )(*([u] * n_slabs), wmat)
    cb = SCAN_BLOCK
    per_chunk = pl.BlockSpec((cb * S5_GROUP_PITCH, 2 * n_state), lambda i: (i, 0))
    x0 = pl.pallas_call(
        _s5_scan_kernel,
        grid=(n_chunks // cb,),
        in_specs=[per_chunk, per_chunk,
                  pl.BlockSpec(a1.shape, lambda i: (0, 0)),
                  pl.BlockSpec(a2.shape, lambda i: (0, 0)),
                  pl.BlockSpec(a2s.shape, lambda i: (0, 0))],
        out_specs=per_chunk,
        out_shape=cg_shape,
        scratch_shapes=[pltpu.VMEM(a1.shape, jnp.float32), pltpu.VMEM(a1.shape, jnp.float32)],
        compiler_params=params, name="s5_scan",
    )(s_a, s_b, a1, a2, a2s)
    return pl.pallas_call(
        functools.partial(_s5_out_kernel, n_slabs=n_slabs, n_ch=n_ch),
        grid=(s_len // rows,),
        in_specs=slab_specs + [blk(cw), cg_rows, const(ktaps), const(emat),
                               const(d), const(wglu), const(bglu)],
        out_specs=pl.BlockSpec((rows, width), lambda i: (i, 0)),
        out_shape=jax.ShapeDtypeStruct((s_len, width), jnp.bfloat16),
        scratch_shapes=([pltpu.VMEM((n_groups, cw, cw), jnp.bfloat16)]
                        + [pltpu.VMEM((rows + CHUNK * S5_STAGE_PAD, lanes), jnp.float32) for _ in range(n_slabs)]
                        + [pltpu.VMEM((rows, lanes), jnp.float32) for _ in range(n_slabs)]),
        compiler_params=params, name="s5_out",
    )(*([u] * n_slabs), u_flat, x0, ktaps, emat, d, wglu, bglu)


def _mix_ffn_kernel(x_ref, attn_ref, ssm_ref, woa_ref, wos_ref, g2_ref, wg_ref, wu_ref, wd_ref, gf_ref,
                    o_ref, act_ref, *, tf):
    mixed = (jnp.dot(attn_ref[...], woa_ref[...], preferred_element_type=jnp.float32)
             + jnp.dot(ssm_ref[...], wos_ref[...], preferred_element_type=jnp.float32))
    h = x_ref[...] + mixed
    hn = _rms(h, g2_ref[...]).astype(jnp.bfloat16)
    for f in range(0, wg_ref.shape[1], tf):
        gate = jnp.dot(hn, wg_ref[:, f:f + tf], preferred_element_type=jnp.float32)
        up = jnp.dot(hn, wu_ref[:, f:f + tf], preferred_element_type=jnp.float32)
        act_ref[:, f:f + tf] = (jax.nn.silu(gate) * up).astype(jnp.bfloat16)
    ffn = jnp.dot(act_ref[...], wd_ref[...], preferred_element_type=jnp.float32)
    o_ref[...] = _rms(h + ffn, gf_ref[...])


def _mix_ffn(x, attn, ssm, wo_attn, wo_ssm, g2, wg, wu, wd, gf):
    s_len, dm = x.shape
    bm = FFN_ROWS
    d_ff = wg.shape[1]
    row = lambda w: pl.BlockSpec((bm, w), lambda i: (i, 0))
    const = lambda a: pl.BlockSpec(a.shape, lambda i: (0, 0), pipeline_mode=pl.Buffered(1))
    return pl.pallas_call(
        functools.partial(_mix_ffn_kernel, tf=FFN_TILE),
        grid=(s_len // bm,),
        in_specs=[row(dm), row(attn.shape[1]), row(ssm.shape[1]),
                  const(wo_attn), const(wo_ssm), const(g2), const(wg), const(wu), const(wd), const(gf)],
        out_specs=row(dm),
        out_shape=jax.ShapeDtypeStruct((s_len, dm), jnp.float32),
        scratch_shapes=[pltpu.VMEM((bm, d_ff), jnp.bfloat16)],
        compiler_params=pltpu.CompilerParams(dimension_semantics=("arbitrary",),
                                             vmem_limit_bytes=V7X_VMEM_LIMIT),
        name="mix_ffn",
    )(x, attn, ssm, wo_attn, wo_ssm, g2, wg, wu, wd, gf)


def kernel(x, positions, norm1_g, w_in, lambda_q1, lambda_k1, lambda_q2, lambda_k2, subln_g, ssm_lambda_re, ssm_lambda_im, ssm_log_step, ssm_b_re, ssm_b_im, ssm_c_re, ssm_c_im, ssm_d, ssm_w_glu, ssm_b_glu, w_out, norm2_g, w_gate, w_up, w_down, final_g):
    bsz, s_len, d_model = x.shape
    assert bsz == 1 and norm1_g.shape[0] == 1, "one sequence, one layer"
    assert s_len % max(IN_PROJ_ROWS, FFN_ROWS, S5_ROWS, CHUNK * SCAN_BLOCK) == 0
    f32, bf16 = jnp.float32, jnp.bfloat16
    head_dim = lambda_q1.shape[-1]
    v_dim = subln_g.shape[-1]
    rot_dim = head_dim // 4
    n_groups, n_state, n_ch = ssm_b_re.shape[1:]
    ssm_width = n_groups * n_ch
    attn_width = d_model - ssm_width
    n_heads = attn_width // v_dim
    qk_width = n_heads * 2 * head_dim
    assert w_in.shape[-1] == 2 * qk_width + attn_width + ssm_width
    assert w_gate.shape[-1] % FFN_TILE == 0

    inv_freq = ROPE_THETA ** (-jnp.arange(0, rot_dim, 2, dtype=f32) / rot_dim)
    invf = jnp.concatenate([inv_freq, inv_freq]).reshape(rot_dim, 1)
    w = w_in[0]
    n_qkv = 2 * qk_width + attn_width
    qT, k, vT, u = _in_proj(x[0], positions[0], norm1_g[0].reshape(1, d_model).astype(f32), invf,
                            w[:, :n_qkv].T.astype(bf16), w[:, n_qkv:].astype(bf16),
                            n_heads=n_heads, head_dim=head_dim, rot_dim=rot_dim, v_dim=v_dim)

    lam_vec = lambda a: a[0].reshape(1, head_dim).astype(f32)
    attn = _attention(lam_vec(lambda_q1), lam_vec(lambda_k1), lam_vec(lambda_q2), lam_vec(lambda_k2),
                      subln_g[0].astype(f32), qT, k, vT, head_dim=head_dim)

    mats = _s5_matrices(ssm_lambda_re[0], ssm_lambda_im[0], ssm_log_step[0], ssm_b_re[0], ssm_b_im[0],
                        ssm_c_re[0], ssm_c_im[0])
    ssm = _s5_mixer(u, *mats, ssm_d[0].reshape(1, ssm_width).astype(f32), ssm_w_glu[0].astype(bf16),
                    ssm_b_glu[0].reshape(1, ssm_width).astype(f32))

    wo = w_out[0].astype(bf16)
    out = _mix_ffn(x[0], attn, ssm, wo[:attn_width], wo[attn_width:],
                   norm2_g[0].reshape(1, d_model).astype(f32),
                   w_gate[0].astype(bf16), w_up[0].astype(bf16), w_down[0].astype(bf16),
                   final_g.reshape(1, d_model).astype(f32))
    return out[None]
```

```python
import functools
import math

import jax
import jax.numpy as jnp
from jax import lax
from jax.experimental import pallas as pl
from jax.experimental.pallas import tpu as pltpu

NORM_EPS = 1e-5
ROPE_THETA = 500000.0
LAMBDA_INIT = 0.8 - 0.6 * math.exp(-0.3 * 0)
LAM_RE_MAX = -1e-4
LOG2_E = math.log2(math.e)
V_EXT_ROWS = 16
CHUNK = 16
V7X_VMEM_LIMIT = 56 * 1024 * 1024
V7X_LANES = 128

IN_PROJ_ROWS = 512
ATTN_BQ = 1024
ATTN_HEADS_PER_STEP = 2
ATTN_TICKS_PER_LOOP = 2
ATTN_ROW_CHUNK = 64
ATTN_MAX_RISE = 100.0
S5_ROWS = 2048
S5_GROUP_PITCH = 40
S5_STAGE_PAD = 8
SCAN_BLOCK = 128
FFN_ROWS = 512
FFN_TILE = 256

_NT = (((1,), (1,)), ((), ()))


def _rms(x, g):
    return x * lax.rsqrt(jnp.mean(x * x, axis=-1, keepdims=True) + NORM_EPS) * g


def _in_proj_kernel(x_ref, pos_ref, g_ref, invf_ref, wT_ref, wu_ref,
                    qT_ref, k_ref, vT_ref, u_ref, *, n_maps, head_dim, rot_dim, qk_width):
    hn = _rms(x_ref[...], g_ref[...]).astype(jnp.bfloat16)
    pT = lax.dot_general(wT_ref[...], hn, _NT, preferred_element_type=jnp.float32)
    u_ref[...] = jnp.dot(hn, wu_ref[...], preferred_element_type=jnp.float32)

    ang = invf_ref[...] * pos_ref[0].astype(jnp.float32)
    cos, sin = jnp.cos(ang), jnp.sin(ang)
    half = rot_dim // 2
    c0, c1, s0, s1 = cos[:half], cos[half:], sin[:half], sin[half:]

    def rope(t, scale):
        rows = []
        for m in range(n_maps):
            r = m * head_dim
            x0, x1 = t[r:r + half], t[r + half:r + rot_dim]
            rows += [x0 * c0 - x1 * s0, x1 * c1 + x0 * s1, t[r + rot_dim:r + head_dim]]
        out = jnp.concatenate(rows, axis=0)
        return out * scale if scale != 1.0 else out

    qT_ref[...] = rope(pT[:qk_width], head_dim ** -0.5 * LOG2_E).astype(jnp.bfloat16)
    n_heads, _, v_rows, bm = vT_ref.shape
    kT = rope(pT[qk_width:2 * qk_width], 1.0)
    for h in range(n_heads):
        k_ref[h] = kT[h * 2 * head_dim:(h + 1) * 2 * head_dim].T.astype(jnp.bfloat16)
    vT = pT[2 * qk_width:].astype(jnp.bfloat16).reshape(n_heads, v_rows - V_EXT_ROWS, bm)
    ext_row = lax.broadcasted_iota(jnp.int32, (n_heads, V_EXT_ROWS, bm), 1)
    ext = jnp.where(ext_row == 0, 1.0, 0.0).astype(jnp.bfloat16)
    vT_ref[...] = jnp.concatenate([vT, ext], axis=1).reshape(vT_ref.shape)


def _in_proj(x, pos, g, invf, wT, wu, *, n_heads, head_dim, rot_dim, v_dim):
    s_len, d = x.shape
    bm = IN_PROJ_ROWS
    qk_width = n_heads * 2 * head_dim
    v_width = n_heads * v_dim
    kern = functools.partial(_in_proj_kernel, n_maps=2 * n_heads, head_dim=head_dim,
                             rot_dim=rot_dim, qk_width=qk_width)
    const = lambda i: (0, 0)
    return pl.pallas_call(
        kern,
        grid=(s_len // bm,),
        in_specs=[pl.BlockSpec((bm, d), lambda i: (i, 0)),
                  pl.BlockSpec((1, 1, bm), lambda i: (i, 0, 0)),
                  pl.BlockSpec((1, d), const),
                  pl.BlockSpec((rot_dim, 1), const),
                  pl.BlockSpec(wT.shape, const),
                  pl.BlockSpec(wu.shape, const)],
        out_specs=[pl.BlockSpec((qk_width, bm), lambda i: (0, i)),
                   pl.BlockSpec((n_heads, bm, 2 * head_dim), lambda i: (0, i, 0)),
                   pl.BlockSpec((n_heads, 1, v_dim + V_EXT_ROWS, bm), lambda i: (0, i, 0, 0)),
                   pl.BlockSpec((bm, wu.shape[1]), lambda i: (i, 0))],
        out_shape=[jax.ShapeDtypeStruct((qk_width, s_len), jnp.bfloat16),
                   jax.ShapeDtypeStruct((n_heads, s_len, 2 * head_dim), jnp.bfloat16),
                   jax.ShapeDtypeStruct((n_heads, s_len // bm, v_dim + V_EXT_ROWS, bm), jnp.bfloat16),
                   jax.ShapeDtypeStruct((s_len, wu.shape[1]), jnp.float32)],
        compiler_params=pltpu.CompilerParams(dimension_semantics=("arbitrary",),
                                             vmem_limit_bytes=V7X_VMEM_LIMIT),
        name="in_proj",
    )(x, pos.reshape(s_len // bm, 1, bm), g, invf, wT, wu)


def _attn_kernel(lq1_ref, lk1_ref, lq2_ref, lk2_ref, g_ref, qT_ref, k_ref, vT_ref, o_ref, *scratch,
                 bq, bk, head_dim, v_dim, n_par):
    scratch = list(scratch)
    take = lambda n: [scratch.pop(0) for _ in range(n)]
    qbd_ref = take(n_par)
    p_ref = [take(n_par) for _ in range(2)]
    m_ref, acc_ref, rise_ref = take(n_par), take(n_par), take(n_par)

    i = pl.program_id(1)
    kw = 2 * head_dim
    n_full = (i * bq) // bk

    zero = jnp.zeros((head_dim, bq), qT_ref.dtype)
    for h in range(n_par):
        qT = qT_ref[h * kw:(h + 1) * kw, :]
        qbd_ref[h][...] = jnp.concatenate([jnp.concatenate([qT[:head_dim], zero], axis=1),
                                           jnp.concatenate([zero, qT[head_dim:]], axis=1)], axis=0)

    def scores(t, h, masked):
        kb = k_ref[h, pl.ds(pl.multiple_of(t * bk, bk), bk), :]
        s = jnp.dot(kb, qbd_ref[h][...], preferred_element_type=jnp.float32)
        if masked:
            kpos = t * bk + lax.broadcasted_iota(jnp.int32, s.shape, 0)
            qcol = lax.broadcasted_iota(jnp.int32, s.shape, 1)
            qpos = i * bq + jnp.where(qcol >= bq, qcol - bq, qcol)
            s = jnp.where(kpos <= qpos, s, -jnp.inf)
        return s

    def pv(t, h, slot):
        return jnp.dot(vT_ref[h, t], p_ref[slot][h][...], preferred_element_type=jnp.float32)

    def max_first_step(t, h, first):
        s = scores(t, h, masked=True)
        m_blk = jnp.max(s, axis=0, keepdims=True)
        m_new = m_blk if first else jnp.maximum(m_ref[h][...], m_blk)
        p_ref[0][h][...] = jnp.exp2(s - m_new).astype(p_ref[0][h].dtype)
        if first:
            acc_ref[h][...] = pv(t, h, 0)
        else:
            acc_ref[h][...] = jnp.exp2(m_ref[h][...] - m_new) * acc_ref[h][...] + pv(t, h, 0)
        m_ref[h][...] = m_new

    def stream_step(t, slot, masked):
        for h in range(n_par):
            r = m_ref[h][...]
            s = scores(t, h, masked)
            m8 = None
            for r0 in range(0, bk, ATTN_ROW_CHUNK):
                c = s[r0:r0 + ATTN_ROW_CHUNK]
                p_ref[slot][h][r0:r0 + ATTN_ROW_CHUNK, :] = jnp.exp2(c - r).astype(p_ref[slot][h].dtype)
                c8 = jnp.max(c.reshape(ATTN_ROW_CHUNK // 8, 8, c.shape[1]), axis=0)
                m8 = c8 if m8 is None else jnp.maximum(m8, c8)
            m_blk = jnp.max(m8, axis=0, keepdims=True)
            m_new = jnp.maximum(r, m_blk)
            rise_ref[h][...] = jnp.maximum(rise_ref[h][...], m_blk - r)
            m_ref[h][...] = m_new
            acc_ref[h][...] = jnp.exp2(r - m_new) * (acc_ref[h][...] + pv(t, h, slot))

    def finish():
        lam = (jnp.exp(jnp.sum(lq1_ref[...] * lk1_ref[...], axis=-1, keepdims=True))
               - jnp.exp(jnp.sum(lq2_ref[...] * lk2_ref[...], axis=-1, keepdims=True))
               + LAMBDA_INIT)
        for h in range(n_par):
            acc = acc_ref[h][...]
            o = acc[:v_dim] / acc[v_dim:v_dim + 1]
            oT = o[:, :bq] - lam * o[:, bq:]
            y = oT * lax.rsqrt(jnp.mean(oT * oT, axis=0, keepdims=True) + NORM_EPS) * g_ref[...]
            o_ref[:, h * v_dim:(h + 1) * v_dim] = (y * (1.0 - LAMBDA_INIT)).T.astype(o_ref.dtype)

    for h in range(n_par):
        max_first_step(0, h, first=True)
        rise_ref[h][...] = jnp.zeros(rise_ref[h].shape, jnp.float32)

    n_stream = jnp.maximum(n_full - 1, 0)

    def group(c, carry):
        t = 1 + ATTN_TICKS_PER_LOOP * c
        for r in range(ATTN_TICKS_PER_LOOP):
            stream_step(t + r, r % 2, masked=False)
        return carry

    lax.fori_loop(0, n_stream // ATTN_TICKS_PER_LOOP, group, 0)

    def single(t, carry):
        stream_step(t, 0, masked=False)
        return carry

    lax.fori_loop(1 + (n_stream // ATTN_TICKS_PER_LOOP) * ATTN_TICKS_PER_LOOP, n_full, single, 0)

    n_diag = -(-bq // bk)
    for d in range(n_diag):
        @pl.when(n_full + d >= 1)
        def _():
            stream_step(n_full + d, (d + 1) % 2, masked=True)

    finish()

    worst = rise_ref[0][...]
    for h in range(1, n_par):
        worst = jnp.maximum(worst, rise_ref[h][...])

    @pl.when(jnp.max(worst) > ATTN_MAX_RISE)
    def _():
        def redo(t, carry):
            for h in range(n_par):
                max_first_step(t, h, first=False)
            return carry

        for h in range(n_par):
            max_first_step(0, h, first=True)
        lax.fori_loop(1, n_full + n_diag, redo, 0)
        finish()


def _attention(lq1, lk1, lq2, lk2, subln_g, qT, k, vT, *, head_dim):
    n_heads, n_kblk, v_rows, bk = vT.shape
    v_dim = v_rows - V_EXT_ROWS
    s_len = k.shape[1]
    bq, n_par = ATTN_BQ, ATTN_HEADS_PER_STEP
    kern = functools.partial(_attn_kernel, bq=bq, bk=bk, head_dim=head_dim, v_dim=v_dim, n_par=n_par)
    vec = pl.BlockSpec((1, head_dim), lambda h, i: (0, 0))
    f32, bf16 = jnp.float32, jnp.bfloat16
    per_head = lambda shape, dtype, n=1: [pltpu.VMEM(shape, dtype) for _ in range(n * n_par)]
    scratch = (per_head((2 * head_dim, 2 * bq), bf16)
               + per_head((bk, 2 * bq), bf16, 2)
               + per_head((1, 2 * bq), f32)
               + per_head((v_rows, 2 * bq), f32)
               + per_head((1, 2 * bq), f32))
    resident = dict(pipeline_mode=pl.Buffered(1))
    return pl.pallas_call(
        kern,
        grid=(n_heads // n_par, s_len // bq),
        in_specs=[vec, vec, vec, vec,
                  pl.BlockSpec((v_dim, 1), lambda h, i: (0, 0)),
                  pl.BlockSpec((n_par * 2 * head_dim, bq), lambda h, i: (h, i)),
                  pl.BlockSpec((n_par, s_len, 2 * head_dim), lambda h, i: (h, 0, 0), **resident),
                  pl.BlockSpec((n_par, n_kblk, v_rows, bk), lambda h, i: (h, 0, 0, 0), **resident)],
        out_specs=pl.BlockSpec((bq, n_par * v_dim), lambda h, i: (i, h)),
        out_shape=jax.ShapeDtypeStruct((s_len, n_heads * v_dim), bf16),
        scratch_shapes=scratch,
        compiler_params=pltpu.CompilerParams(dimension_semantics=("arbitrary", "arbitrary"),
                                             vmem_limit_bytes=V7X_VMEM_LIMIT),
        name="attn",
    )(lq1, lk1, lq2, lk2, subln_g.reshape(v_dim, 1), qT, k, vT)


def _s5_matrices(lam_re, lam_im, log_step, b_re, b_im, c_re, c_im):
    f32 = jnp.float32
    n_groups, n_state, n_ch = b_re.shape
    lr = jnp.minimum(lam_re.astype(f32), LAM_RE_MAX)
    li = lam_im.astype(f32)
    step = jnp.exp(log_step.astype(f32))[:, None]
    mag = jnp.exp(lr * step)
    lb_re, lb_im = mag * jnp.cos(li * step), mag * jnp.sin(li * step)
    denom = lr * lr + li * li
    nr, ni = lb_re - 1.0, lb_im
    coef_re = (nr * lr + ni * li) / denom
    coef_im = (ni * lr - nr * li) / denom
    br, bi = b_re.astype(f32), b_im.astype(f32)
    bb_re = coef_re[..., None] * br - coef_im[..., None] * bi
    bb_im = coef_re[..., None] * bi + coef_im[..., None] * br
    tau = jnp.arange(CHUNK + 1, dtype=f32)[None, :, None]
    pmag = jnp.exp(lr[:, None, :] * step[:, None, :] * tau)
    pw_re = pmag * jnp.cos(li[:, None, :] * step[:, None, :] * tau)
    pw_im = pmag * jnp.sin(li[:, None, :] * step[:, None, :] * tau)
    cr, ci = c_re.astype(f32)[:, None], c_im.astype(f32)[:, None]
    cp_re = cr * pw_re[:, :, None, :] - ci * pw_im[:, :, None, :]
    cp_im = cr * pw_im[:, :, None, :] + ci * pw_re[:, :, None, :]
    kmat = jnp.sum(cp_re[..., None] * bb_re[:, None, None] - cp_im[..., None] * bb_im[:, None, None], axis=3)
    ktaps = kmat[:, :CHUNK].transpose(0, 3, 1, 2).reshape(n_groups, n_ch, CHUNK * n_ch)
    rev = CHUNK - 1 - jnp.arange(CHUNK)
    ar, ai = pw_re[:, rev][:, :, None, :], pw_im[:, rev][:, :, None, :]
    brt, bit = bb_re.transpose(0, 2, 1)[:, None], bb_im.transpose(0, 2, 1)[:, None]
    w_re = (ar * brt - ai * bit).reshape(n_groups, CHUNK * n_ch, n_state)
    w_im = (ar * bit + ai * brt).reshape(n_groups, CHUNK * n_ch, n_state)
    wmat = jnp.concatenate([w_re, w_im, w_im, w_re], axis=-1)
    e_re = cp_re[:, 1:].transpose(0, 3, 1, 2).reshape(n_groups, n_state, CHUNK * n_ch)
    e_im = cp_im[:, 1:].transpose(0, 3, 1, 2).reshape(n_groups, n_state, CHUNK * n_ch)
    emat = jnp.concatenate([e_re, -e_im], axis=1)
    al_re, al_im = pw_re[:, CHUNK], pw_im[:, CHUNK]
    a1 = jnp.concatenate([al_re, al_re], axis=-1)
    a2 = jnp.concatenate([-al_im, al_im], axis=-1)
    bf = jnp.bfloat16
    return ktaps, wmat.astype(bf), emat.astype(bf), a1, a2, -a2


def _transpose_pieces(arrs, width):
    n = len(arrs)
    piece = lax.broadcasted_iota(jnp.int32, arrs[0].shape, 1) // width
    arrs = list(arrs)
    d = n // 2
    while d:
        upper = (piece // d) % 2 == 1
        for a in range(n):
            if a & d == 0:
                lo, hi = arrs[a], arrs[a + d]
                arrs[a] = jnp.where(upper, pltpu.roll(hi, d * width, axis=1), lo)
                arrs[a + d] = jnp.where(upper, hi, pltpu.roll(lo, (n - d) * width, axis=1))
        d //= 2
    return arrs


def _chunk_rows_to_lanes(u_refs, n_cb, n_ch):
    gps = u_refs[0].shape[1] // n_ch
    out = []
    for u_ref in u_refs:
        rows = [u_ref[pl.ds(j, n_cb, stride=CHUNK), :] for j in range(CHUNK)]
        cols = [_transpose_pieces(rows[m * gps:(m + 1) * gps], n_ch) for m in range(CHUNK // gps)]
        out += [jnp.concatenate([c[gl] for c in cols], axis=1) for gl in range(gps)]
    return out


def _chunk_lanes_to_rows(ys, stage_refs, o_refs, n_cb, n_ch):
    lanes = o_refs[0].shape[1]
    gps = lanes // n_ch
    pitch = stage_refs[0].shape[0] // CHUNK
    for slab, stage_ref in enumerate(stage_refs):
        for m in range(CHUNK // gps):
            group_cols = [ys[slab * gps + gl][:, m * lanes:(m + 1) * lanes] for gl in range(gps)]
            for r, x in enumerate(_transpose_pieces(group_cols, n_ch)):
                t = m * gps + r
                stage_ref[t * pitch:t * pitch + n_cb, :] = x

    def chunk_rows(c, carry):
        for stage_ref, o_ref in zip(stage_refs, o_refs):
            o_ref[pl.ds(pl.multiple_of(c * CHUNK, CHUNK), CHUNK), :] = stage_ref[pl.ds(c, CHUNK, stride=pitch), :]
        return carry

    lax.fori_loop(0, n_cb, chunk_rows, 0, unroll=8)


def _s5_state_kernel(*refs, n_slabs, n_ch):
    u_refs, (w_ref, uf_ref, sa_ref, sb_ref) = refs[:n_slabs], refs[n_slabs:]
    flat = _chunk_rows_to_lanes(u_refs, uf_ref.shape[1], n_ch)
    n_groups, n_cb, _ = uf_ref.shape
    half = sa_ref.shape[1]
    for g, f in enumerate(flat):
        f = f.astype(jnp.bfloat16)
        uf_ref[g] = f
        s = jnp.dot(f, w_ref[g], preferred_element_type=jnp.float32)
        sa_ref[pl.ds(g, n_cb, stride=S5_GROUP_PITCH), :] = s[:, :half]
        sb_ref[pl.ds(g, n_cb, stride=S5_GROUP_PITCH), :] = s[:, half:]
    for g in range(n_groups, S5_GROUP_PITCH):
        sa_ref[pl.ds(g, n_cb, stride=S5_GROUP_PITCH), :] = jnp.zeros((n_cb, half), jnp.float32)
        sb_ref[pl.ds(g, n_cb, stride=S5_GROUP_PITCH), :] = jnp.zeros((n_cb, half), jnp.float32)


def _s5_scan_kernel(sa_ref, sb_ref, a1_ref, a2_ref, a2s_ref, x0_ref, x_ref, xs_ref):
    @pl.when(pl.program_id(0) == 0)
    def _():
        x_ref[...] = jnp.zeros(x_ref.shape, jnp.float32)
        xs_ref[...] = jnp.zeros(xs_ref.shape, jnp.float32)

    a1, a2, a2s = a1_ref[...], a2_ref[...], a2s_ref[...]
    n_groups = a1.shape[0]
    pad = jnp.zeros((S5_GROUP_PITCH - n_groups, a1.shape[1]), jnp.float32)

    def body(c, carry):
        x, xs = carry
        row = pl.multiple_of(c * S5_GROUP_PITCH, 8)
        x0_ref[pl.ds(row, n_groups), :] = x
        x0_ref[pl.ds(row + n_groups, S5_GROUP_PITCH - n_groups), :] = pad
        return (a1 * x + a2 * xs + sa_ref[pl.ds(row, n_groups), :],
                a1 * xs + a2s * x + sb_ref[pl.ds(row, n_groups), :])

    n_chunks = sa_ref.shape[0] // S5_GROUP_PITCH
    x, xs = lax.fori_loop(0, n_chunks, body, (x_ref[...], xs_ref[...]), unroll=4)
    x_ref[...] = x
    xs_ref[...] = xs


def _s5_out_kernel(*refs, n_slabs, n_ch):
    u_refs = refs[:n_slabs]
    uf_ref, x0_ref, kt_ref, e_ref, d_ref, wglu_ref, bglu_ref, o_ref, t_ref = refs[n_slabs:n_slabs + 9]
    stage_refs, y_refs = refs[n_slabs + 9:2 * n_slabs + 9], refs[2 * n_slabs + 9:]
    n_groups, n_cb, cw = uf_ref.shape

    @pl.when(pl.program_id(0) == 0)
    def _():
        lane = lax.broadcasted_iota(jnp.int32, (n_ch, cw), 1)
        for g in range(n_groups):
            taps = kt_ref[g]
            for j in range(CHUNK):
                rows = taps if j == 0 else jnp.where(lane >= j * n_ch,
                                                     pltpu.roll(taps, j * n_ch, axis=1), 0.0)
                t_ref[g, j * n_ch:(j + 1) * n_ch, :] = rows.astype(t_ref.dtype)

    ys = [jnp.dot(uf_ref[g], t_ref[g], preferred_element_type=jnp.float32)
          + jnp.dot(x0_ref[pl.ds(g, n_cb, stride=S5_GROUP_PITCH), :].astype(jnp.bfloat16), e_ref[g],
                    preferred_element_type=jnp.float32)
          for g in range(n_groups)]
    _chunk_lanes_to_rows(ys, stage_refs, y_refs, n_cb, n_ch)
    y = jnp.concatenate([y_ref[...] for y_ref in y_refs], axis=1)
    u = jnp.concatenate([u_ref[...] for u_ref in u_refs], axis=1)
    y = jax.nn.gelu(y + d_ref[...] * u)
    gate = jnp.dot(y.astype(jnp.bfloat16), wglu_ref[...],
                   preferred_element_type=jnp.float32) + bglu_ref[...]
    o_ref[...] = (y * jax.nn.sigmoid(gate)).astype(o_ref.dtype)


def _s5_mixer(u, ktaps, wmat, emat, a1, a2, a2s, d, wglu, bglu):
    s_len, width = u.shape
    n_groups, n_ch, cw = ktaps.shape
    n_state = a1.shape[1] // 2
    n_chunks = s_len // CHUNK
    rows = S5_ROWS
    n_cb = rows // CHUNK
    lanes = V7X_LANES
    n_slabs = width // lanes
    params = pltpu.CompilerParams(dimension_semantics=("arbitrary",), vmem_limit_bytes=V7X_VMEM_LIMIT)
    slab_specs = [pl.BlockSpec((rows, lanes), functools.partial(lambda k, i: (i, k), k))
                  for k in range(n_slabs)]
    const = lambda a: pl.BlockSpec(a.shape, lambda i: (0,) * a.ndim, pipeline_mode=pl.Buffered(1))
    blk = lambda w: pl.BlockSpec((n_groups, n_cb, w), lambda i: (0, i, 0))
    assert n_groups <= S5_GROUP_PITCH and S5_GROUP_PITCH % 8 == 0
    cg_rows = pl.BlockSpec((n_cb * S5_GROUP_PITCH, 2 * n_state), lambda i: (i, 0))
    cg_shape = jax.ShapeDtypeStruct((n_chunks * S5_GROUP_PITCH, 2 * n_state), jnp.float32)
    u_flat, s_a, s_b = pl.pallas_call(
        functools.partial(_s5_state_kernel, n_slabs=n_slabs, n_ch=n_ch),
        grid=(s_len // rows,),
        in_specs=slab_specs + [const(wmat)],
        out_specs=[blk(cw), cg_rows, cg_rows],
        out_shape=[jax.ShapeDtypeStruct((n_groups, n_chunks, cw), jnp.bfloat16), cg_shape, cg_shape],
        compiler_params=params, name="s5_state",
    )(*([u] * n_slabs), wmat)
    cb = SCAN_BLOCK
    per_chunk = pl.BlockSpec((cb * S5_GROUP_PITCH, 2 * n_state), lambda i: (i, 0))
    x0 = pl.pallas_call(
        _s5_scan_kernel,
        grid=(n_chunks // cb,),
        in_specs=[per_chunk, per_chunk,
                  pl.BlockSpec(a1.shape, lambda i: (0, 0)),
                  pl.BlockSpec(a2.shape, lambda i: (0, 0)),
                  pl.BlockSpec(a2s.shape, lambda i: (0, 0))],
        out_specs=per_chunk,
        out_shape=cg_shape,
        scratch_shapes=[pltpu.VMEM(a1.shape, jnp.float32), pltpu.VMEM(a1.shape, jnp.float32)],
        compiler_params=params, name="s5_scan",
    )(s_a, s_b, a1, a2, a2s)
    return pl.pallas_call(
        functools.partial(_s5_out_kernel, n_slabs=n_slabs, n_ch=n_ch),
        grid=(s_len // rows,),
        in_specs=slab_specs + [blk(cw), cg_rows, const(ktaps), const(emat),
                               const(d), const(wglu), const(bglu)],
        out_specs=pl.BlockSpec((rows, width), lambda i: (i, 0)),
        out_shape=jax.ShapeDtypeStruct((s_len, width), jnp.bfloat16),
        scratch_shapes=([pltpu.VMEM((n_groups, cw, cw), jnp.bfloat16)]
                        + [pltpu.VMEM((rows + CHUNK * S5_STAGE_PAD, lanes), jnp.float32) for _ in range(n_slabs)]
                        + [pltpu.VMEM((rows, lanes), jnp.float32) for _ in range(n_slabs)]),
        compiler_params=params, name="s5_out",
    )(*([u] * n_slabs), u_flat, x0, ktaps, emat, d, wglu, bglu)


def _mix_ffn_kernel(x_ref, attn_ref, ssm_ref, woa_ref, wos_ref, g2_ref, wg_ref, wu_ref, wd_ref, gf_ref,
                    o_ref, act_ref, *, tf):
    mixed = (jnp.dot(attn_ref[...], woa_ref[...], preferred_element_type=jnp.float32)
             + jnp.dot(ssm_ref[...], wos_ref[...], preferred_element_type=jnp.float32))
    h = x_ref[...] + mixed
    hn = _rms(h, g2_ref[...]).astype(jnp.bfloat16)
    for f in range(0, wg_ref.shape[1], tf):
        gate = jnp.dot(hn, wg_ref[:, f:f + tf], preferred_element_type=jnp.float32)
        up = jnp.dot(hn, wu_ref[:, f:f + tf], preferred_element_type=jnp.float32)
        act_ref[:, f:f + tf] = (jax.nn.silu(gate) * up).astype(jnp.bfloat16)
    ffn = jnp.dot(act_ref[...], wd_ref[...], preferred_element_type=jnp.float32)
    o_ref[...] = _rms(h + ffn, gf_ref[...])


def _mix_ffn(x, attn, ssm, wo_attn, wo_ssm, g2, wg, wu, wd, gf):
    s_len, dm = x.shape
    bm = FFN_ROWS
    d_ff = wg.shape[1]
    row = lambda w: pl.BlockSpec((bm, w), lambda i: (i, 0))
    const = lambda a: pl.BlockSpec(a.shape, lambda i: (0, 0), pipeline_mode=pl.Buffered(1))
    return pl.pallas_call(
        functools.partial(_mix_ffn_kernel, tf=FFN_TILE),
        grid=(s_len // bm,),
        in_specs=[row(dm), row(attn.shape[1]), row(ssm.shape[1]),
                  const(wo_attn), const(wo_ssm), const(g2), const(wg), const(wu), const(wd), const(gf)],
        out_specs=row(dm),
        out_shape=jax.ShapeDtypeStruct((s_len, dm), jnp.float32),
        scratch_shapes=[pltpu.VMEM((bm, d_ff), jnp.bfloat16)],
        compiler_params=pltpu.CompilerParams(dimension_semantics=("arbitrary",),
                                             vmem_limit_bytes=V7X_VMEM_LIMIT),
        name="mix_ffn",
    )(x, attn, ssm, wo_attn, wo_ssm, g2, wg, wu, wd, gf)


def kernel(x, positions, norm1_g, w_in, lambda_q1, lambda_k1, lambda_q2, lambda_k2, subln_g, ssm_lambda_re, ssm_lambda_im, ssm_log_step, ssm_b_re, ssm_b_im, ssm_c_re, ssm_c_im, ssm_d, ssm_w_glu, ssm_b_glu, w_out, norm2_g, w_gate, w_up, w_down, final_g):
    bsz, s_len, d_model = x.shape
    assert bsz == 1 and norm1_g.shape[0] == 1, "one sequence, one layer"
    assert s_len % max(IN_PROJ_ROWS, FFN_ROWS, S5_ROWS, CHUNK * SCAN_BLOCK) == 0
    f32, bf16 = jnp.float32, jnp.bfloat16
    head_dim = lambda_q1.shape[-1]
    v_dim = subln_g.shape[-1]
    rot_dim = head_dim // 4
    n_groups, n_state, n_ch = ssm_b_re.shape[1:]
    ssm_width = n_groups * n_ch
    attn_width = d_model - ssm_width
    n_heads = attn_width // v_dim
    qk_width = n_heads * 2 * head_dim
    assert w_in.shape[-1] == 2 * qk_width + attn_width + ssm_width
    assert w_gate.shape[-1] % FFN_TILE == 0

    inv_freq = ROPE_THETA ** (-jnp.arange(0, rot_dim, 2, dtype=f32) / rot_dim)
    invf = jnp.concatenate([inv_freq, inv_freq]).reshape(rot_dim, 1)
    w = w_in[0]
    n_qkv = 2 * qk_width + attn_width
    qT, k, vT, u = _in_proj(x[0], positions[0], norm1_g[0].reshape(1, d_model).astype(f32), invf,
                            w[:, :n_qkv].T.astype(bf16), w[:, n_qkv:].astype(bf16),
                            n_heads=n_heads, head_dim=head_dim, rot_dim=rot_dim, v_dim=v_dim)

    lam_vec = lambda a: a[0].reshape(1, head_dim).astype(f32)
    attn = _attention(lam_vec(lambda_q1), lam_vec(lambda_k1), lam_vec(lambda_q2), lam_vec(lambda_k2),
                      subln_g[0].astype(f32), qT, k, vT, head_dim=head_dim)

    mats = _s5_matrices(ssm_lambda_re[0], ssm_lambda_im[0], ssm_log_step[0], ssm_b_re[0], ssm_b_im[0],
                        ssm_c_re[0], ssm_c_im[0])
    ssm = _s5_mixer(u, *mats, ssm_d[0].reshape(1, ssm_width).astype(f32), ssm_w_glu[0].astype(bf16),
                    ssm_b_glu[0].reshape(1, ssm_width).astype(f32))

    wo = w_out[0].astype(bf16)
    out = _mix_ffn(x[0], attn, ssm, wo[:attn_width], wo[attn_width:],
                   norm2_g[0].reshape(1, d_model).astype(f32),
                   w_gate[0].astype(bf16), w_up[0].astype(bf16), w_down[0].astype(bf16),
                   final_g.reshape(1, d_model).astype(f32))
    return out[None]
```

```python
import functools
import math

import jax
import jax.numpy as jnp
from jax import lax
from jax.experimental import pallas as pl
from jax.experimental.pallas import tpu as pltpu

NORM_EPS = 1e-5
ROPE_THETA = 500000.0
LAMBDA_INIT = 0.8 - 0.6 * math.exp(-0.3 * 0)
LAM_RE_MAX = -1e-4
LOG2_E = math.log2(math.e)
V_EXT_ROWS = 16
CHUNK = 16
V7X_VMEM_LIMIT = 56 * 1024 * 1024
V7X_LANES = 128

IN_PROJ_ROWS = 1024
ATTN_BQ = 1024
ATTN_HEADS_PER_STEP = 2
ATTN_TICKS_PER_LOOP = 2
ATTN_ROW_CHUNK = 64
ATTN_MAX_RISE = 100.0
S5_ROWS = 2048
S5_GROUP_PITCH = 40
S5_STAGE_PAD = 8
SCAN_BLOCK = 128
FFN_ROWS = 1024
FFN_TILE = 256

_NT = (((1,), (1,)), ((), ()))


def _rms(x, g):
    return x * lax.rsqrt(jnp.mean(x * x, axis=-1, keepdims=True) + NORM_EPS) * g


def _in_proj_kernel(x_ref, pos_ref, g_ref, invf_ref, wT_ref, wu_ref,
                    qT_ref, k_ref, vT_ref, u_ref, *, n_maps, head_dim, rot_dim, qk_width):
    hn = _rms(x_ref[...], g_ref[...]).astype(jnp.bfloat16)
    pT = lax.dot_general(wT_ref[...], hn, _NT, preferred_element_type=jnp.float32)
    u_ref[...] = jnp.dot(hn, wu_ref[...], preferred_element_type=jnp.float32)

    ang = invf_ref[...] * pos_ref[0].astype(jnp.float32)
    cos, sin = jnp.cos(ang), jnp.sin(ang)
    half = rot_dim // 2
    c0, c1, s0, s1 = cos[:half], cos[half:], sin[:half], sin[half:]

    def rope(t, scale):
        rows = []
        for m in range(n_maps):
            r = m * head_dim
            x0, x1 = t[r:r + half], t[r + half:r + rot_dim]
            rows += [x0 * c0 - x1 * s0, x1 * c1 + x0 * s1, t[r + rot_dim:r + head_dim]]
        out = jnp.concatenate(rows, axis=0)
        return out * scale if scale != 1.0 else out

    qT_ref[...] = rope(pT[:qk_width], head_dim ** -0.5 * LOG2_E).astype(jnp.bfloat16)
    n_heads, _, v_rows, bm = vT_ref.shape
    kT = rope(pT[qk_width:2 * qk_width], 1.0)
    for h in range(n_heads):
        k_ref[h] = kT[h * 2 * head_dim:(h + 1) * 2 * head_dim].T.astype(jnp.bfloat16)
    vT = pT[2 * qk_width:].astype(jnp.bfloat16).reshape(n_heads, v_rows - V_EXT_ROWS, bm)
    ext_row = lax.broadcasted_iota(jnp.int32, (n_heads, V_EXT_ROWS, bm), 1)
    ext = jnp.where(ext_row == 0, 1.0, 0.0).astype(jnp.bfloat16)
    vT_ref[...] = jnp.concatenate([vT, ext], axis=1).reshape(vT_ref.shape)


def _in_proj(x, pos, g, invf, wT, wu, *, n_heads, head_dim, rot_dim, v_dim):
    s_len, d = x.shape
    bm = IN_PROJ_ROWS
    qk_width = n_heads * 2 * head_dim
    v_width = n_heads * v_dim
    kern = functools.partial(_in_proj_kernel, n_maps=2 * n_heads, head_dim=head_dim,
                             rot_dim=rot_dim, qk_width=qk_width)
    const = lambda i: (0, 0)
    return pl.pallas_call(
        kern,
        grid=(s_len // bm,),
        in_specs=[pl.BlockSpec((bm, d), lambda i: (i, 0)),
                  pl.BlockSpec((1, 1, bm), lambda i: (i, 0, 0)),
                  pl.BlockSpec((1, d), const),
                  pl.BlockSpec((rot_dim, 1), const),
                  pl.BlockSpec(wT.shape, const),
                  pl.BlockSpec(wu.shape, const)],
        out_specs=[pl.BlockSpec((qk_width, bm), lambda i: (0, i)),
                   pl.BlockSpec((n_heads, bm, 2 * head_dim), lambda i: (0, i, 0)),
                   pl.BlockSpec((n_heads, 1, v_dim + V_EXT_ROWS, bm), lambda i: (0, i, 0, 0)),
                   pl.BlockSpec((bm, wu.shape[1]), lambda i: (i, 0))],
        out_shape=[jax.ShapeDtypeStruct((qk_width, s_len), jnp.bfloat16),
                   jax.ShapeDtypeStruct((n_heads, s_len, 2 * head_dim), jnp.bfloat16),
                   jax.ShapeDtypeStruct((n_heads, s_len // bm, v_dim + V_EXT_ROWS, bm), jnp.bfloat16),
                   jax.ShapeDtypeStruct((s_len, wu.shape[1]), jnp.float32)],
        compiler_params=pltpu.CompilerParams(dimension_semantics=("arbitrary",),
                                             vmem_limit_bytes=V7X_VMEM_LIMIT),
        name="in_proj",
    )(x, pos.reshape(s_len // bm, 1, bm), g, invf, wT, wu)


def _attn_kernel(lq1_ref, lk1_ref, lq2_ref, lk2_ref, g_ref, qT_ref, k_ref, vT_ref, o_ref, *scratch,
                 bq, bk, head_dim, v_dim, n_par):
    scratch = list(scratch)
    take = lambda n: [scratch.pop(0) for _ in range(n)]
    qbd_ref = take(n_par)
    p_ref = [take(n_par) for _ in range(2)]
    m_ref, acc_ref, rise_ref = take(n_par), take(n_par), take(n_par)

    i = pl.program_id(1)
    kw = 2 * head_dim
    n_full = (i * bq) // bk

    zero = jnp.zeros((head_dim, bq), qT_ref.dtype)
    for h in range(n_par):
        qT = qT_ref[h * kw:(h + 1) * kw, :]
        qbd_ref[h][...] = jnp.concatenate([jnp.concatenate([qT[:head_dim], zero], axis=1),
                                           jnp.concatenate([zero, qT[head_dim:]], axis=1)], axis=0)

    def scores(t, h, masked):
        kb = k_ref[h, pl.ds(pl.multiple_of(t * bk, bk), bk), :]
        s = jnp.dot(kb, qbd_ref[h][...], preferred_element_type=jnp.float32)
        if masked:
            kpos = t * bk + lax.broadcasted_iota(jnp.int32, s.shape, 0)
            qcol = lax.broadcasted_iota(jnp.int32, s.shape, 1)
            qpos = i * bq + jnp.where(qcol >= bq, qcol - bq, qcol)
            s = jnp.where(kpos <= qpos, s, -jnp.inf)
        return s

    def pv(t, h, slot):
        return jnp.dot(vT_ref[h, t], p_ref[slot][h][...], preferred_element_type=jnp.float32)

    def max_first_step(t, h, first):
        s = scores(t, h, masked=True)
        m_blk = jnp.max(s, axis=0, keepdims=True)
        m_new = m_blk if first else jnp.maximum(m_ref[h][...], m_blk)
        p_ref[0][h][...] = jnp.exp2(s - m_new).astype(p_ref[0][h].dtype)
        if first:
            acc_ref[h][...] = pv(t, h, 0)
        else:
            acc_ref[h][...] = jnp.exp2(m_ref[h][...] - m_new) * acc_ref[h][...] + pv(t, h, 0)
        m_ref[h][...] = m_new

    def stream_step(t, slot, masked):
        for h in range(n_par):
            r = m_ref[h][...]
            s = scores(t, h, masked)
            m8 = None
            for r0 in range(0, bk, ATTN_ROW_CHUNK):
                c = s[r0:r0 + ATTN_ROW_CHUNK]
                p_ref[slot][h][r0:r0 + ATTN_ROW_CHUNK, :] = jnp.exp2(c - r).astype(p_ref[slot][h].dtype)
                c8 = jnp.max(c.reshape(ATTN_ROW_CHUNK // 8, 8, c.shape[1]), axis=0)
                m8 = c8 if m8 is None else jnp.maximum(m8, c8)
            m_blk = jnp.max(m8, axis=0, keepdims=True)
            m_new = jnp.maximum(r, m_blk)
            rise_ref[h][...] = jnp.maximum(rise_ref[h][...], m_blk - r)
            m_ref[h][...] = m_new
            acc_ref[h][...] = jnp.exp2(r - m_new) * (acc_ref[h][...] + pv(t, h, slot))

    def finish():
        lam = (jnp.exp(jnp.sum(lq1_ref[...] * lk1_ref[...], axis=-1, keepdims=True))
               - jnp.exp(jnp.sum(lq2_ref[...] * lk2_ref[...], axis=-1, keepdims=True))
               + LAMBDA_INIT)
        for h in range(n_par):
            acc = acc_ref[h][...]
            o = acc[:v_dim] / acc[v_dim:v_dim + 1]
            oT = o[:, :bq] - lam * o[:, bq:]
            y = oT * lax.rsqrt(jnp.mean(oT * oT, axis=0, keepdims=True) + NORM_EPS) * g_ref[...]
            o_ref[:, h * v_dim:(h + 1) * v_dim] = (y * (1.0 - LAMBDA_INIT)).T.astype(o_ref.dtype)

    for h in range(n_par):
        max_first_step(0, h, first=True)
        rise_ref[h][...] = jnp.zeros(rise_ref[h].shape, jnp.float32)

    n_stream = jnp.maximum(n_full - 1, 0)

    def group(c, carry):
        t = 1 + ATTN_TICKS_PER_LOOP * c
        for r in range(ATTN_TICKS_PER_LOOP):
            stream_step(t + r, r % 2, masked=False)
        return carry

    lax.fori_loop(0, n_stream // ATTN_TICKS_PER_LOOP, group, 0)

    def single(t, carry):
        stream_step(t, 0, masked=False)
        return carry

    lax.fori_loop(1 + (n_stream // ATTN_TICKS_PER_LOOP) * ATTN_TICKS_PER_LOOP, n_full, single, 0)

    n_diag = -(-bq // bk)
    for d in range(n_diag):
        @pl.when(n_full + d >= 1)
        def _():
            stream_step(n_full + d, (d + 1) % 2, masked=True)

    finish()

    worst = rise_ref[0][...]
    for h in range(1, n_par):
        worst = jnp.maximum(worst, rise_ref[h][...])

    @pl.when(jnp.max(worst) > ATTN_MAX_RISE)
    def _():
        def redo(t, carry):
            for h in range(n_par):
                max_first_step(t, h, first=False)
            return carry

        for h in range(n_par):
            max_first_step(0, h, first=True)
        lax.fori_loop(1, n_full + n_diag, redo, 0)
        finish()


def _attention(lq1, lk1, lq2, lk2, subln_g, qT, k, vT, *, head_dim):
    n_heads, n_kblk, v_rows, bk = vT.shape
    v_dim = v_rows - V_EXT_ROWS
    s_len = k.shape[1]
    bq, n_par = ATTN_BQ, ATTN_HEADS_PER_STEP
    kern = functools.partial(_attn_kernel, bq=bq, bk=bk, head_dim=head_dim, v_dim=v_dim, n_par=n_par)
    vec = pl.BlockSpec((1, head_dim), lambda h, i: (0, 0))
    f32, bf16 = jnp.float32, jnp.bfloat16
    per_head = lambda shape, dtype, n=1: [pltpu.VMEM(shape, dtype) for _ in range(n * n_par)]
    scratch = (per_head((2 * head_dim, 2 * bq), bf16)
               + per_head((bk, 2 * bq), bf16, 2)
               + per_head((1, 2 * bq), f32)
               + per_head((v_rows, 2 * bq), f32)
               + per_head((1, 2 * bq), f32))
    resident = dict(pipeline_mode=pl.Buffered(1))
    return pl.pallas_call(
        kern,
        grid=(n_heads // n_par, s_len // bq),
        in_specs=[vec, vec, vec, vec,
                  pl.BlockSpec((v_dim, 1), lambda h, i: (0, 0)),
                  pl.BlockSpec((n_par * 2 * head_dim, bq), lambda h, i: (h, i)),
                  pl.BlockSpec((n_par, s_len, 2 * head_dim), lambda h, i: (h, 0, 0), **resident),
                  pl.BlockSpec((n_par, n_kblk, v_rows, bk), lambda h, i: (h, 0, 0, 0), **resident)],
        out_specs=pl.BlockSpec((bq, n_par * v_dim), lambda h, i: (i, h)),
        out_shape=jax.ShapeDtypeStruct((s_len, n_heads * v_dim), bf16),
        scratch_shapes=scratch,
        compiler_params=pltpu.CompilerParams(dimension_semantics=("arbitrary", "arbitrary"),
                                             vmem_limit_bytes=V7X_VMEM_LIMIT),
        name="attn",
    )(lq1, lk1, lq2, lk2, subln_g.reshape(v_dim, 1), qT, k, vT)


def _s5_matrices(lam_re, lam_im, log_step, b_re, b_im, c_re, c_im):
    f32 = jnp.float32
    n_groups, n_state, n_ch = b_re.shape
    lr = jnp.minimum(lam_re.astype(f32), LAM_RE_MAX)
    li = lam_im.astype(f32)
    step = jnp.exp(log_step.astype(f32))[:, None]
    mag = jnp.exp(lr * step)
    lb_re, lb_im = mag * jnp.cos(li * step), mag * jnp.sin(li * step)
    denom = lr * lr + li * li
    nr, ni = lb_re - 1.0, lb_im
    coef_re = (nr * lr + ni * li) / denom
    coef_im = (ni * lr - nr * li) / denom
    br, bi = b_re.astype(f32), b_im.astype(f32)
    bb_re = coef_re[..., None] * br - coef_im[..., None] * bi
    bb_im = coef_re[..., None] * bi + coef_im[..., None] * br
    tau = jnp.arange(CHUNK + 1, dtype=f32)[None, :, None]
    pmag = jnp.exp(lr[:, None, :] * step[:, None, :] * tau)
    pw_re = pmag * jnp.cos(li[:, None, :] * step[:, None, :] * tau)
    pw_im = pmag * jnp.sin(li[:, None, :] * step[:, None, :] * tau)
    cr, ci = c_re.astype(f32)[:, None], c_im.astype(f32)[:, None]
    cp_re = cr * pw_re[:, :, None, :] - ci * pw_im[:, :, None, :]
    cp_im = cr * pw_im[:, :, None, :] + ci * pw_re[:, :, None, :]
    kmat = jnp.sum(cp_re[..., None] * bb_re[:, None, None] - cp_im[..., None] * bb_im[:, None, None], axis=3)
    ktaps = kmat[:, :CHUNK].transpose(0, 3, 1, 2).reshape(n_groups, n_ch, CHUNK * n_ch)
    rev = CHUNK - 1 - jnp.arange(CHUNK)
    ar, ai = pw_re[:, rev][:, :, None, :], pw_im[:, rev][:, :, None, :]
    brt, bit = bb_re.transpose(0, 2, 1)[:, None], bb_im.transpose(0, 2, 1)[:, None]
    w_re = (ar * brt - ai * bit).reshape(n_groups, CHUNK * n_ch, n_state)
    w_im = (ar * bit + ai * brt).reshape(n_groups, CHUNK * n_ch, n_state)
    wmat = jnp.concatenate([w_re, w_im, w_im, w_re], axis=-1)
    e_re = cp_re[:, 1:].transpose(0, 3, 1, 2).reshape(n_groups, n_state, CHUNK * n_ch)
    e_im = cp_im[:, 1:].transpose(0, 3, 1, 2).reshape(n_groups, n_state, CHUNK * n_ch)
    emat = jnp.concatenate([e_re, -e_im], axis=1)
    al_re, al_im = pw_re[:, CHUNK], pw_im[:, CHUNK]
    a1 = jnp.concatenate([al_re, al_re], axis=-1)
    a2 = jnp.concatenate([-al_im, al_im], axis=-1)
    bf = jnp.bfloat16
    return ktaps, wmat.astype(bf), emat.astype(bf), a1, a2, -a2


def _transpose_pieces(arrs, width):
    n = len(arrs)
    piece = lax.broadcasted_iota(jnp.int32, arrs[0].shape, 1) // width
    arrs = list(arrs)
    d = n // 2
    while d:
        upper = (piece // d) % 2 == 1
        for a in range(n):
            if a & d == 0:
                lo, hi = arrs[a], arrs[a + d]
                arrs[a] = jnp.where(upper, pltpu.roll(hi, d * width, axis=1), lo)
                arrs[a + d] = jnp.where(upper, hi, pltpu.roll(lo, (n - d) * width, axis=1))
        d //= 2
    return arrs


def _chunk_rows_to_lanes(u_refs, n_cb, n_ch):
    gps = u_refs[0].shape[1] // n_ch
    out = []
    for u_ref in u_refs:
        rows = [u_ref[pl.ds(j, n_cb, stride=CHUNK), :] for j in range(CHUNK)]
        cols = [_transpose_pieces(rows[m * gps:(m + 1) * gps], n_ch) for m in range(CHUNK // gps)]
        out += [jnp.concatenate([c[gl] for c in cols], axis=1) for gl in range(gps)]
    return out


def _chunk_lanes_to_rows(ys, stage_refs, o_refs, n_cb, n_ch):
    lanes = o_refs[0].shape[1]
    gps = lanes // n_ch
    pitch = stage_refs[0].shape[0] // CHUNK
    for slab, stage_ref in enumerate(stage_refs):
        for m in range(CHUNK // gps):
            group_cols = [ys[slab * gps + gl][:, m * lanes:(m + 1) * lanes] for gl in range(gps)]
            for r, x in enumerate(_transpose_pieces(group_cols, n_ch)):
                t = m * gps + r
                stage_ref[t * pitch:t * pitch + n_cb, :] = x

    def chunk_rows(c, carry):
        for stage_ref, o_ref in zip(stage_refs, o_refs):
            o_ref[pl.ds(pl.multiple_of(c * CHUNK, CHUNK), CHUNK), :] = stage_ref[pl.ds(c, CHUNK, stride=pitch), :]
        return carry

    lax.fori_loop(0, n_cb, chunk_rows, 0, unroll=8)


def _s5_state_kernel(*refs, n_slabs, n_ch):
    u_refs, (w_ref, uf_ref, sa_ref, sb_ref) = refs[:n_slabs], refs[n_slabs:]
    flat = _chunk_rows_to_lanes(u_refs, uf_ref.shape[1], n_ch)
    n_groups, n_cb, _ = uf_ref.shape
    half = sa_ref.shape[1]
    for g, f in enumerate(flat):
        f = f.astype(jnp.bfloat16)
        uf_ref[g] = f
        s = jnp.dot(f, w_ref[g], preferred_element_type=jnp.float32)
        sa_ref[pl.ds(g, n_cb, stride=S5_GROUP_PITCH), :] = s[:, :half]
        sb_ref[pl.ds(g, n_cb, stride=S5_GROUP_PITCH), :] = s[:, half:]
    for g in range(n_groups, S5_GROUP_PITCH):
        sa_ref[pl.ds(g, n_cb, stride=S5_GROUP_PITCH), :] = jnp.zeros((n_cb, half), jnp.float32)
        sb_ref[pl.ds(g, n_cb, stride=S5_GROUP_PITCH), :] = jnp.zeros((n_cb, half), jnp.float32)


def _s5_scan_kernel(sa_ref, sb_ref, a1_ref, a2_ref, a2s_ref, x0_ref, x_ref, xs_ref):
    @pl.when(pl.program_id(0) == 0)
    def _():
        x_ref[...] = jnp.zeros(x_ref.shape, jnp.float32)
        xs_ref[...] = jnp.zeros(xs_ref.shape, jnp.float32)

    a1, a2, a2s = a1_ref[...], a2_ref[...], a2s_ref[...]
    n_groups = a1.shape[0]
    pad = jnp.zeros((S5_GROUP_PITCH - n_groups, a1.shape[1]), jnp.float32)

    def body(c, carry):
        x, xs = carry
        row = pl.multiple_of(c * S5_GROUP_PITCH, 8)
        x0_ref[pl.ds(row, n_groups), :] = x
        x0_ref[pl.ds(row + n_groups, S5_GROUP_PITCH - n_groups), :] = pad
        return (a1 * x + a2 * xs + sa_ref[pl.ds(row, n_groups), :],
                a1 * xs + a2s * x + sb_ref[pl.ds(row, n_groups), :])

    n_chunks = sa_ref.shape[0] // S5_GROUP_PITCH
    x, xs = lax.fori_loop(0, n_chunks, body, (x_ref[...], xs_ref[...]), unroll=4)
    x_ref[...] = x
    xs_ref[...] = xs


def _s5_out_kernel(*refs, n_slabs, n_ch):
    u_refs = refs[:n_slabs]
    uf_ref, x0_ref, kt_ref, e_ref, d_ref, wglu_ref, bglu_ref, o_ref, t_ref = refs[n_slabs:n_slabs + 9]
    stage_refs, y_refs = refs[n_slabs + 9:2 * n_slabs + 9], refs[2 * n_slabs + 9:]
    n_groups, n_cb, cw = uf_ref.shape

    @pl.when(pl.program_id(0) == 0)
    def _():
        lane = lax.broadcasted_iota(jnp.int32, (n_ch, cw), 1)
        for g in range(n_groups):
            taps = kt_ref[g]
            for j in range(CHUNK):
                rows = taps if j == 0 else jnp.where(lane >= j * n_ch,
                                                     pltpu.roll(taps, j * n_ch, axis=1), 0.0)
                t_ref[g, j * n_ch:(j + 1) * n_ch, :] = rows.astype(t_ref.dtype)

    ys = [jnp.dot(uf_ref[g], t_ref[g], preferred_element_type=jnp.float32)
          + jnp.dot(x0_ref[pl.ds(g, n_cb, stride=S5_GROUP_PITCH), :].astype(jnp.bfloat16), e_ref[g],
                    preferred_element_type=jnp.float32)
          for g in range(n_groups)]
    _chunk_lanes_to_rows(ys, stage_refs, y_refs, n_cb, n_ch)
    y = jnp.concatenate([y_ref[...] for y_ref in y_refs], axis=1)
    u = jnp.concatenate([u_ref[...] for u_ref in u_refs], axis=1)
    y = jax.nn.gelu(y + d_ref[...] * u)
    gate = jnp.dot(y.astype(jnp.bfloat16), wglu_ref[...],
                   preferred_element_type=jnp.float32) + bglu_ref[...]
    o_ref[...] = (y * jax.nn.sigmoid(gate)).astype(o_ref.dtype)


def _s5_mixer(u, ktaps, wmat, emat, a1, a2, a2s, d, wglu, bglu):
    s_len, width = u.shape
    n_groups, n_ch, cw = ktaps.shape
    n_state = a1.shape[1] // 2
    n_chunks = s_len // CHUNK
    rows = S5_ROWS
    n_cb = rows // CHUNK
    lanes = V7X_LANES
    n_slabs = width // lanes
    params = pltpu.CompilerParams(dimension_semantics=("arbitrary",), vmem_limit_bytes=V7X_VMEM_LIMIT)
    slab_specs = [pl.BlockSpec((rows, lanes), functools.partial(lambda k, i: (i, k), k))
                  for k in range(n_slabs)]
    const = lambda a: pl.BlockSpec(a.shape, lambda i: (0,) * a.ndim, pipeline_mode=pl.Buffered(1))
    blk = lambda w: pl.BlockSpec((n_groups, n_cb, w), lambda i: (0, i, 0))
    assert n_groups <= S5_GROUP_PITCH and S5_GROUP_PITCH % 8 == 0
    cg_rows = pl.BlockSpec((n_cb * S5_GROUP_PITCH, 2 * n_state), lambda i: (i, 0))
    cg_shape = jax.ShapeDtypeStruct((n_chunks * S5_GROUP_PITCH, 2 * n_state), jnp.float32)
    u_flat, s_a, s_b = pl.pallas_call(
        functools.partial(_s5_state_kernel, n_slabs=n_slabs, n_ch=n_ch),
        grid=(s_len // rows,),
        in_specs=slab_specs + [const(wmat)],
        out_specs=[blk(cw), cg_rows, cg_rows],
        out_shape=[jax.ShapeDtypeStruct((n_groups, n_chunks, cw), jnp.bfloat16), cg_shape, cg_shape],
        compiler_params=params, name="s5_state",
    )(*([u] * n_slabs), wmat)
    cb = SCAN_BLOCK
    per_chunk = pl.BlockSpec((cb * S5_GROUP_PITCH, 2 * n_state), lambda i: (i, 0))
    x0 = pl.pallas_call(
        _s5_scan_kernel,
        grid=(n_chunks // cb,),
        in_specs=[per_chunk, per_chunk,
                  pl.BlockSpec(a1.shape, lambda i: (0, 0)),
                  pl.BlockSpec(a2.shape, lambda i: (0, 0)),
                  pl.BlockSpec(a2s.shape, lambda i: (0, 0))],
        out_specs=per_chunk,
        out_shape=cg_shape,
        scratch_shapes=[pltpu.VMEM(a1.shape, jnp.float32), pltpu.VMEM(a1.shape, jnp.float32)],
        compiler_params=params, name="s5_scan",
    )(s_a, s_b, a1, a2, a2s)
    return pl.pallas_call(
        functools.partial(_s5_out_kernel, n_slabs=n_slabs, n_ch=n_ch),
        grid=(s_len // rows,),
        in_specs=slab_specs + [blk(cw), cg_rows, const(ktaps), const(emat),
                               const(d), const(wglu), const(bglu)],
        out_specs=pl.BlockSpec((rows, width), lambda i: (i, 0)),
        out_shape=jax.ShapeDtypeStruct((s_len, width), jnp.bfloat16),
        scratch_shapes=([pltpu.VMEM((n_groups, cw, cw), jnp.bfloat16)]
                        + [pltpu.VMEM((rows + CHUNK * S5_STAGE_PAD, lanes), jnp.float32) for _ in range(n_slabs)]
                        + [pltpu.VMEM((rows, lanes), jnp.float32) for _ in range(n_slabs)]),
        compiler_params=params, name="s5_out",
    )(*([u] * n_slabs), u_flat, x0, ktaps, emat, d, wglu, bglu)


def _mix_ffn_kernel(x_ref, attn_ref, ssm_ref, woa_ref, wos_ref, g2_ref, wg_ref, wu_ref, wd_ref, gf_ref,
                    o_ref, act_ref, *, tf):
    mixed = (jnp.dot(attn_ref[...], woa_ref[...], preferred_element_type=jnp.float32)
             + jnp.dot(ssm_ref[...], wos_ref[...], preferred_element_type=jnp.float32))
    h = x_ref[...] + mixed
    hn = _rms(h, g2_ref[...]).astype(jnp.bfloat16)
    for f in range(0, wg_ref.shape[1], tf):
        gate = jnp.dot(hn, wg_ref[:, f:f + tf], preferred_element_type=jnp.float32)
        up = jnp.dot(hn, wu_ref[:, f:f + tf], preferred_element_type=jnp.float32)
        act_ref[:, f:f + tf] = (jax.nn.silu(gate) * up).astype(jnp.bfloat16)
    ffn = jnp.dot(act_ref[...], wd_ref[...], preferred_element_type=jnp.float32)
    o_ref[...] = _rms(h + ffn, gf_ref[...])


def _mix_ffn(x, attn, ssm, wo_attn, wo_ssm, g2, wg, wu, wd, gf):
    s_len, dm = x.shape
    bm = FFN_ROWS
    d_ff = wg.shape[1]
    row = lambda w: pl.BlockSpec((bm, w), lambda i: (i, 0))
    const = lambda a: pl.BlockSpec(a.shape, lambda i: (0, 0), pipeline_mode=pl.Buffered(1))
    return pl.pallas_call(
        functools.partial(_mix_ffn_kernel, tf=FFN_TILE),
        grid=(s_len // bm,),
        in_specs=[row(dm), row(attn.shape[1]), row(ssm.shape[1]),
                  const(wo_attn), const(wo_ssm), const(g2), const(wg), const(wu), const(wd), const(gf)],
        out_specs=row(dm),
        out_shape=jax.ShapeDtypeStruct((s_len, dm), jnp.float32),
        scratch_shapes=[pltpu.VMEM((bm, d_ff), jnp.bfloat16)],
        compiler_params=pltpu.CompilerParams(dimension_semantics=("arbitrary",),
                                             vmem_limit_bytes=V7X_VMEM_LIMIT),
        name="mix_ffn",
    )(x, attn, ssm, wo_attn, wo_ssm, g2, wg, wu, wd, gf)


def kernel(x, positions, norm1_g, w_in, lambda_q1, lambda_k1, lambda_q2, lambda_k2, subln_g, ssm_lambda_re, ssm_lambda_im, ssm_log_step, ssm_b_re, ssm_b_im, ssm_c_re, ssm_c_im, ssm_d, ssm_w_glu, ssm_b_glu, w_out, norm2_g, w_gate, w_up, w_down, final_g):
    bsz, s_len, d_model = x.shape
    assert bsz == 1 and norm1_g.shape[0] == 1, "one sequence, one layer"
    assert s_len % max(IN_PROJ_ROWS, FFN_ROWS, S5_ROWS, CHUNK * SCAN_BLOCK) == 0
    f32, bf16 = jnp.float32, jnp.bfloat16
    head_dim = lambda_q1.shape[-1]
    v_dim = subln_g.shape[-1]
    rot_dim = head_dim // 4
    n_groups, n_state, n_ch = ssm_b_re.shape[1:]
    ssm_width = n_groups * n_ch
    attn_width = d_model - ssm_width
    n_heads = attn_width // v_dim
    qk_width = n_heads * 2 * head_dim
    assert w_in.shape[-1] == 2 * qk_width + attn_width + ssm_width
    assert w_gate.shape[-1] % FFN_TILE == 0

    inv_freq = ROPE_THETA ** (-jnp.arange(0, rot_dim, 2, dtype=f32) / rot_dim)
    invf = jnp.concatenate([inv_freq, inv_freq]).reshape(rot_dim, 1)
    w = w_in[0]
    n_qkv = 2 * qk_width + attn_width
    qT, k, vT, u = _in_proj(x[0], positions[0], norm1_g[0].reshape(1, d_model).astype(f32), invf,
                            w[:, :n_qkv].T.astype(bf16), w[:, n_qkv:].astype(bf16),
                            n_heads=n_heads, head_dim=head_dim, rot_dim=rot_dim, v_dim=v_dim)

    lam_vec = lambda a: a[0].reshape(1, head_dim).astype(f32)
    attn = _attention(lam_vec(lambda_q1), lam_vec(lambda_k1), lam_vec(lambda_q2), lam_vec(lambda_k2),
                      subln_g[0].astype(f32), qT, k, vT, head_dim=head_dim)

    mats = _s5_matrices(ssm_lambda_re[0], ssm_lambda_im[0], ssm_log_step[0], ssm_b_re[0], ssm_b_im[0],
                        ssm_c_re[0], ssm_c_im[0])
    ssm = _s5_mixer(u, *mats, ssm_d[0].reshape(1, ssm_width).astype(f32), ssm_w_glu[0].astype(bf16),
                    ssm_b_glu[0].reshape(1, ssm_width).astype(f32))

    wo = w_out[0].astype(bf16)
    out = _mix_ffn(x[0], attn, ssm, wo[:attn_width], wo[attn_width:],
                   norm2_g[0].reshape(1, d_model).astype(f32),
                   w_gate[0].astype(bf16), w_up[0].astype(bf16), w_down[0].astype(bf16),
                   final_g.reshape(1, d_model).astype(f32))
    return out[None]
```

```python
import functools
import math

import jax
import jax.numpy as jnp
from jax import lax
from jax.experimental import pallas as pl
from jax.experimental.pallas import tpu as pltpu

NORM_EPS = 1e-5
ROPE_THETA = 500000.0
LAMBDA_INIT = 0.8 - 0.6 * math.exp(-0.3 * 0)
LAM_RE_MAX = -1e-4
LOG2_E = math.log2(math.e)
V_EXT_ROWS = 16
CHUNK = 16
V7X_VMEM_LIMIT = 56 * 1024 * 1024
V7X_LANES = 128

IN_PROJ_ROWS = 1024
ATTN_BQ = 1024
ATTN_HEADS_PER_STEP = 2
ATTN_TICKS_PER_LOOP = 2
ATTN_ROW_CHUNK = 64
ATTN_MAX_RISE = 100.0
S5_ROWS = 2048
S5_GROUP_PITCH = 40
S5_STAGE_PAD = 8
SCAN_BLOCK = 128
FFN_ROWS = 1024
FFN_TILE = 256

_NT = (((1,), (1,)), ((), ()))


def _rms(x, g):
    return x * lax.rsqrt(jnp.mean(x * x, axis=-1, keepdims=True) + NORM_EPS) * g


def _in_proj_kernel(x_ref, pos_ref, g_ref, invf_ref, wT_ref, wu_ref,
                    qT_ref, k_ref, vT_ref, u_ref, *, n_maps, head_dim, rot_dim, qk_width):
    hn = _rms(x_ref[...], g_ref[...]).astype(jnp.bfloat16)
    pT = lax.dot_general(wT_ref[...], hn, _NT, preferred_element_type=jnp.float32)
    u_ref[...] = jnp.dot(hn, wu_ref[...], preferred_element_type=jnp.float32)

    ang = invf_ref[...] * pos_ref[0].astype(jnp.float32)
    cos, sin = jnp.cos(ang), jnp.sin(ang)
    half = rot_dim // 2
    c0, c1, s0, s1 = cos[:half], cos[half:], sin[:half], sin[half:]

    def rope(t, scale):
        rows = []
        for m in range(n_maps):
            r = m * head_dim
            x0, x1 = t[r:r + half], t[r + half:r + rot_dim]
            rows += [x0 * c0 - x1 * s0, x1 * c1 + x0 * s1, t[r + rot_dim:r + head_dim]]
        out = jnp.concatenate(rows, axis=0)
        return out * scale if scale != 1.0 else out

    qT_ref[...] = rope(pT[:qk_width], head_dim ** -0.5 * LOG2_E).astype(jnp.bfloat16)
    n_heads, _, v_rows, bm = vT_ref.shape
    kT = rope(pT[qk_width:2 * qk_width], 1.0)
    for h in range(n_heads):
        k_ref[h] = kT[h * 2 * head_dim:(h + 1) * 2 * head_dim].T.astype(jnp.bfloat16)
    vT = pT[2 * qk_width:].astype(jnp.bfloat16).reshape(n_heads, v_rows - V_EXT_ROWS, bm)
    ext_row = lax.broadcasted_iota(jnp.int32, (n_heads, V_EXT_ROWS, bm), 1)
    ext = jnp.where(ext_row == 0, 1.0, 0.0).astype(jnp.bfloat16)
    vT_ref[...] = jnp.concatenate([vT, ext], axis=1).reshape(vT_ref.shape)


def _in_proj(x, pos, g, invf, wT, wu, *, n_heads, head_dim, rot_dim, v_dim):
    s_len, d = x.shape
    bm = IN_PROJ_ROWS
    qk_width = n_heads * 2 * head_dim
    v_width = n_heads * v_dim
    kern = functools.partial(_in_proj_kernel, n_maps=2 * n_heads, head_dim=head_dim,
                             rot_dim=rot_dim, qk_width=qk_width)
    const = lambda i: (0, 0)
    return pl.pallas_call(
        kern,
        grid=(s_len // bm,),
        in_specs=[pl.BlockSpec((bm, d), lambda i: (i, 0)),
                  pl.BlockSpec((1, 1, bm), lambda i: (i, 0, 0)),
                  pl.BlockSpec((1, d), const),
                  pl.BlockSpec((rot_dim, 1), const),
                  pl.BlockSpec(wT.shape, const),
                  pl.BlockSpec(wu.shape, const)],
        out_specs=[pl.BlockSpec((qk_width, bm), lambda i: (0, i)),
                   pl.BlockSpec((n_heads, bm, 2 * head_dim), lambda i: (0, i, 0)),
                   pl.BlockSpec((n_heads, 1, v_dim + V_EXT_ROWS, bm), lambda i: (0, i, 0, 0)),
                   pl.BlockSpec((bm, wu.shape[1]), lambda i: (i, 0))],
        out_shape=[jax.ShapeDtypeStruct((qk_width, s_len), jnp.bfloat16),
                   jax.ShapeDtypeStruct((n_heads, s_len, 2 * head_dim), jnp.bfloat16),
                   jax.ShapeDtypeStruct((n_heads, s_len // bm, v_dim + V_EXT_ROWS, bm), jnp.bfloat16),
                   jax.ShapeDtypeStruct((s_len, wu.shape[1]), jnp.float32)],
        compiler_params=pltpu.CompilerParams(dimension_semantics=("arbitrary",),
                                             vmem_limit_bytes=V7X_VMEM_LIMIT),
        name="in_proj",
    )(x, pos.reshape(s_len // bm, 1, bm), g, invf, wT, wu)


def _attn_kernel(lq1_ref, lk1_ref, lq2_ref, lk2_ref, g_ref, qT_ref, k_ref, vT_ref, o_ref, *scratch,
                 bq, bk, head_dim, v_dim, n_par):
    scratch = list(scratch)
    take = lambda n: [scratch.pop(0) for _ in range(n)]
    qbd_ref = take(n_par)
    p_ref = [take(n_par) for _ in range(2)]
    m_ref, acc_ref, rise_ref = take(n_par), take(n_par), take(n_par)

    i = pl.program_id(1)
    kw = 2 * head_dim
    n_full = (i * bq) // bk

    zero = jnp.zeros((head_dim, bq), qT_ref.dtype)
    for h in range(n_par):
        qT = qT_ref[h * kw:(h + 1) * kw, :]
        qbd_ref[h][...] = jnp.concatenate([jnp.concatenate([qT[:head_dim], zero], axis=1),
                                           jnp.concatenate([zero, qT[head_dim:]], axis=1)], axis=0)

    def scores(t, h, masked):
        kb = k_ref[h, pl.ds(pl.multiple_of(t * bk, bk), bk), :]
        s = jnp.dot(kb, qbd_ref[h][...], preferred_element_type=jnp.float32)
        if masked:
            kpos = t * bk + lax.broadcasted_iota(jnp.int32, s.shape, 0)
            qcol = lax.broadcasted_iota(jnp.int32, s.shape, 1)
            qpos = i * bq + jnp.where(qcol >= bq, qcol - bq, qcol)
            s = jnp.where(kpos <= qpos, s, -jnp.inf)
        return s

    def pv(t, h, slot):
        return jnp.dot(vT_ref[h, t], p_ref[slot][h][...], preferred_element_type=jnp.float32)

    def max_first_step(t, h, first):
        s = scores(t, h, masked=True)
        m_blk = jnp.max(s, axis=0, keepdims=True)
        m_new = m_blk if first else jnp.maximum(m_ref[h][...], m_blk)
        p_ref[0][h][...] = jnp.exp2(s - m_new).astype(p_ref[0][h].dtype)
        if first:
            acc_ref[h][...] = pv(t, h, 0)
        else:
            acc_ref[h][...] = jnp.exp2(m_ref[h][...] - m_new) * acc_ref[h][...] + pv(t, h, 0)
        m_ref[h][...] = m_new

    def stream_step(t, slot, masked):
        for h in range(n_par):
            r = m_ref[h][...]
            s = scores(t, h, masked)
            m8 = None
            for r0 in range(0, bk, ATTN_ROW_CHUNK):
                c = s[r0:r0 + ATTN_ROW_CHUNK]
                p_ref[slot][h][r0:r0 + ATTN_ROW_CHUNK, :] = jnp.exp2(c - r).astype(p_ref[slot][h].dtype)
                c8 = jnp.max(c.reshape(ATTN_ROW_CHUNK // 8, 8, c.shape[1]), axis=0)
                m8 = c8 if m8 is None else jnp.maximum(m8, c8)
            m_blk = jnp.max(m8, axis=0, keepdims=True)
            m_new = jnp.maximum(r, m_blk)
            rise_ref[h][...] = jnp.maximum(rise_ref[h][...], m_blk - r)
            m_ref[h][...] = m_new
            acc_ref[h][...] = jnp.exp2(r - m_new) * (acc_ref[h][...] + pv(t, h, slot))

    def finish():
        lam = (jnp.exp(jnp.sum(lq1_ref[...] * lk1_ref[...], axis=-1, keepdims=True))
               - jnp.exp(jnp.sum(lq2_ref[...] * lk2_ref[...], axis=-1, keepdims=True))
               + LAMBDA_INIT)
        for h in range(n_par):
            acc = acc_ref[h][...]
            o = acc[:v_dim] / acc[v_dim:v_dim + 1]
            oT = o[:, :bq] - lam * o[:, bq:]
            y = oT * lax.rsqrt(jnp.mean(oT * oT, axis=0, keepdims=True) + NORM_EPS) * g_ref[...]
            o_ref[:, h * v_dim:(h + 1) * v_dim] = (y * (1.0 - LAMBDA_INIT)).T.astype(o_ref.dtype)

    for h in range(n_par):
        max_first_step(0, h, first=True)
        rise_ref[h][...] = jnp.zeros(rise_ref[h].shape, jnp.float32)

    n_stream = jnp.maximum(n_full - 1, 0)

    def group(c, carry):
        t = 1 + ATTN_TICKS_PER_LOOP * c
        for r in range(ATTN_TICKS_PER_LOOP):
            stream_step(t + r, r % 2, masked=False)
        return carry

    lax.fori_loop(0, n_stream // ATTN_TICKS_PER_LOOP, group, 0)

    def single(t, carry):
        stream_step(t, 0, masked=False)
        return carry

    lax.fori_loop(1 + (n_stream // ATTN_TICKS_PER_LOOP) * ATTN_TICKS_PER_LOOP, n_full, single, 0)

    def stream_part(t, slot, key_off, key_len, q_off, q_len):
        lane_sets = [slice(c * bq + q_off, c * bq + q_off + q_len) for c in range(2)]
        part = lambda ref: jnp.concatenate([ref[:, ls] for ls in lane_sets], axis=1)
        for h in range(n_par):
            kb = k_ref[h, pl.ds(pl.multiple_of(t * bk + key_off, key_len), key_len), :]
            s = jnp.dot(kb, part(qbd_ref[h]), preferred_element_type=jnp.float32)
            kpos = t * bk + key_off + lax.broadcasted_iota(jnp.int32, s.shape, 0)
            qcol = lax.broadcasted_iota(jnp.int32, s.shape, 1)
            qpos = i * bq + q_off + jnp.where(qcol >= q_len, qcol - q_len, qcol)
            s = jnp.where(kpos <= qpos, s, -jnp.inf)
            r = part(m_ref[h])
            p_ref[slot][h][0:key_len, 0:2 * q_len] = jnp.exp2(s - r).astype(p_ref[slot][h].dtype)
            m_blk = jnp.max(s, axis=0, keepdims=True)
            m_new = jnp.maximum(r, m_blk)
            rise = jnp.maximum(part(rise_ref[h]), m_blk - r)
            pv = jnp.dot(vT_ref[h, t][:, key_off:key_off + key_len], p_ref[slot][h][0:key_len, 0:2 * q_len],
                         preferred_element_type=jnp.float32)
            acc = jnp.exp2(r - m_new) * (part(acc_ref[h]) + pv)
            for c, ls in enumerate(lane_sets):
                half = slice(c * q_len, (c + 1) * q_len)
                m_ref[h][:, ls] = m_new[:, half]
                rise_ref[h][:, ls] = rise[:, half]
                acc_ref[h][:, ls] = acc[:, half]

    n_diag = -(-bq // bk)
    if bq == bk:
        @pl.when(n_full >= 1)
        def _():
            stream_part(n_full, 1, 0, bk // 2, 0, bq)
            stream_part(n_full, 0, bk // 2, bk // 2, bq // 2, bq // 2)
    else:
        for d in range(n_diag):
            @pl.when(n_full + d >= 1)
            def _():
                stream_step(n_full + d, (d + 1) % 2, masked=True)

    finish()

    worst = rise_ref[0][...]
    for h in range(1, n_par):
        worst = jnp.maximum(worst, rise_ref[h][...])

    @pl.when(jnp.max(worst) > ATTN_MAX_RISE)
    def _():
        def redo(t, carry):
            for h in range(n_par):
                max_first_step(t, h, first=False)
            return carry

        for h in range(n_par):
            max_first_step(0, h, first=True)
        lax.fori_loop(1, n_full + n_diag, redo, 0)
        finish()


def _attention(lq1, lk1, lq2, lk2, subln_g, qT, k, vT, *, head_dim):
    n_heads, n_kblk, v_rows, bk = vT.shape
    v_dim = v_rows - V_EXT_ROWS
    s_len = k.shape[1]
    bq, n_par = ATTN_BQ, ATTN_HEADS_PER_STEP
    kern = functools.partial(_attn_kernel, bq=bq, bk=bk, head_dim=head_dim, v_dim=v_dim, n_par=n_par)
    vec = pl.BlockSpec((1, head_dim), lambda h, i: (0, 0))
    f32, bf16 = jnp.float32, jnp.bfloat16
    per_head = lambda shape, dtype, n=1: [pltpu.VMEM(shape, dtype) for _ in range(n * n_par)]
    scratch = (per_head((2 * head_dim, 2 * bq), bf16)
               + per_head((bk, 2 * bq), bf16, 2)
               + per_head((1, 2 * bq), f32)
               + per_head((v_rows, 2 * bq), f32)
               + per_head((1, 2 * bq), f32))
    resident = dict(pipeline_mode=pl.Buffered(1))
    return pl.pallas_call(
        kern,
        grid=(n_heads // n_par, s_len // bq),
        in_specs=[vec, vec, vec, vec,
                  pl.BlockSpec((v_dim, 1), lambda h, i: (0, 0)),
                  pl.BlockSpec((n_par * 2 * head_dim, bq), lambda h, i: (h, i)),
                  pl.BlockSpec((n_par, s_len, 2 * head_dim), lambda h, i: (h, 0, 0), **resident),
                  pl.BlockSpec((n_par, n_kblk, v_rows, bk), lambda h, i: (h, 0, 0, 0), **resident)],
        out_specs=pl.BlockSpec((bq, n_par * v_dim), lambda h, i: (i, h)),
        out_shape=jax.ShapeDtypeStruct((s_len, n_heads * v_dim), bf16),
        scratch_shapes=scratch,
        compiler_params=pltpu.CompilerParams(dimension_semantics=("arbitrary", "arbitrary"),
                                             vmem_limit_bytes=V7X_VMEM_LIMIT),
        name="attn",
    )(lq1, lk1, lq2, lk2, subln_g.reshape(v_dim, 1), qT, k, vT)


def _s5_matrices(lam_re, lam_im, log_step, b_re, b_im, c_re, c_im):
    f32 = jnp.float32
    n_groups, n_state, n_ch = b_re.shape
    lr = jnp.minimum(lam_re.astype(f32), LAM_RE_MAX)
    li = lam_im.astype(f32)
    step = jnp.exp(log_step.astype(f32))[:, None]
    mag = jnp.exp(lr * step)
    lb_re, lb_im = mag * jnp.cos(li * step), mag * jnp.sin(li * step)
    denom = lr * lr + li * li
    nr, ni = lb_re - 1.0, lb_im
    coef_re = (nr * lr + ni * li) / denom
    coef_im = (ni * lr - nr * li) / denom
    br, bi = b_re.astype(f32), b_im.astype(f32)
    bb_re = coef_re[..., None] * br - coef_im[..., None] * bi
    bb_im = coef_re[..., None] * bi + coef_im[..., None] * br
    tau = jnp.arange(CHUNK + 1, dtype=f32)[None, :, None]
    pmag = jnp.exp(lr[:, None, :] * step[:, None, :] * tau)
    pw_re = pmag * jnp.cos(li[:, None, :] * step[:, None, :] * tau)
    pw_im = pmag * jnp.sin(li[:, None, :] * step[:, None, :] * tau)
    cr, ci = c_re.astype(f32)[:, None], c_im.astype(f32)[:, None]
    cp_re = cr * pw_re[:, :, None, :] - ci * pw_im[:, :, None, :]
    cp_im = cr * pw_im[:, :, None, :] + ci * pw_re[:, :, None, :]
    kmat = jnp.sum(cp_re[..., None] * bb_re[:, None, None] - cp_im[..., None] * bb_im[:, None, None], axis=3)
    ktaps = kmat[:, :CHUNK].transpose(0, 3, 1, 2).reshape(n_groups, n_ch, CHUNK * n_ch)
    rev = CHUNK - 1 - jnp.arange(CHUNK)
    ar, ai = pw_re[:, rev][:, :, None, :], pw_im[:, rev][:, :, None, :]
    brt, bit = bb_re.transpose(0, 2, 1)[:, None], bb_im.transpose(0, 2, 1)[:, None]
    w_re = (ar * brt - ai * bit).reshape(n_groups, CHUNK * n_ch, n_state)
    w_im = (ar * bit + ai * brt).reshape(n_groups, CHUNK * n_ch, n_state)
    wmat = jnp.concatenate([w_re, w_im, w_im, w_re], axis=-1)
    e_re = cp_re[:, 1:].transpose(0, 3, 1, 2).reshape(n_groups, n_state, CHUNK * n_ch)
    e_im = cp_im[:, 1:].transpose(0, 3, 1, 2).reshape(n_groups, n_state, CHUNK * n_ch)
    emat = jnp.concatenate([e_re, -e_im], axis=1)
    al_re, al_im = pw_re[:, CHUNK], pw_im[:, CHUNK]
    a1 = jnp.concatenate([al_re, al_re], axis=-1)
    a2 = jnp.concatenate([-al_im, al_im], axis=-1)
    bf = jnp.bfloat16
    return ktaps, wmat.astype(bf), emat.astype(bf), a1, a2, -a2


def _transpose_pieces(arrs, width):
    n = len(arrs)
    piece = lax.broadcasted_iota(jnp.int32, arrs[0].shape, 1) // width
    arrs = list(arrs)
    d = n // 2
    while d:
        upper = (piece // d) % 2 == 1
        for a in range(n):
            if a & d == 0:
                lo, hi = arrs[a], arrs[a + d]
                arrs[a] = jnp.where(upper, pltpu.roll(hi, d * width, axis=1), lo)
                arrs[a + d] = jnp.where(upper, hi, pltpu.roll(lo, (n - d) * width, axis=1))
        d //= 2
    return arrs


def _chunk_rows_to_lanes(u_refs, n_cb, n_ch):
    gps = u_refs[0].shape[1] // n_ch
    out = []
    for u_ref in u_refs:
        rows = [u_ref[pl.ds(j, n_cb, stride=CHUNK), :] for j in range(CHUNK)]
        cols = [_transpose_pieces(rows[m * gps:(m + 1) * gps], n_ch) for m in range(CHUNK // gps)]
        out += [jnp.concatenate([c[gl] for c in cols], axis=1) for gl in range(gps)]
    return out


def _chunk_lanes_to_rows(ys, stage_refs, o_refs, n_cb, n_ch):
    lanes = o_refs[0].shape[1]
    gps = lanes // n_ch
    pitch = stage_refs[0].shape[0] // CHUNK
    for slab, stage_ref in enumerate(stage_refs):
        for m in range(CHUNK // gps):
            group_cols = [ys[slab * gps + gl][:, m * lanes:(m + 1) * lanes] for gl in range(gps)]
            for r, x in enumerate(_transpose_pieces(group_cols, n_ch)):
                t = m * gps + r
                stage_ref[t * pitch:t * pitch + n_cb, :] = x

    def chunk_rows(c, carry):
        for stage_ref, o_ref in zip(stage_refs, o_refs):
            o_ref[pl.ds(pl.multiple_of(c * CHUNK, CHUNK), CHUNK), :] = stage_ref[pl.ds(c, CHUNK, stride=pitch), :]
        return carry

    lax.fori_loop(0, n_cb, chunk_rows, 0, unroll=8)


def _s5_state_kernel(*refs, n_slabs, n_ch):
    u_refs, (w_ref, uf_ref, sa_ref, sb_ref) = refs[:n_slabs], refs[n_slabs:]
    flat = _chunk_rows_to_lanes(u_refs, uf_ref.shape[1], n_ch)
    n_groups, n_cb, _ = uf_ref.shape
    half = sa_ref.shape[1]
    for g, f in enumerate(flat):
        f = f.astype(jnp.bfloat16)
        uf_ref[g] = f
        s = jnp.dot(f, w_ref[g], preferred_element_type=jnp.float32)
        sa_ref[pl.ds(g, n_cb, stride=S5_GROUP_PITCH), :] = s[:, :half]
        sb_ref[pl.ds(g, n_cb, stride=S5_GROUP_PITCH), :] = s[:, half:]
    for g in range(n_groups, S5_GROUP_PITCH):
        sa_ref[pl.ds(g, n_cb, stride=S5_GROUP_PITCH), :] = jnp.zeros((n_cb, half), jnp.float32)
        sb_ref[pl.ds(g, n_cb, stride=S5_GROUP_PITCH), :] = jnp.zeros((n_cb, half), jnp.float32)


def _s5_scan_kernel(sa_ref, sb_ref, a1_ref, a2_ref, a2s_ref, x0_ref, x_ref, xs_ref):
    @pl.when(pl.program_id(0) == 0)
    def _():
        x_ref[...] = jnp.zeros(x_ref.shape, jnp.float32)
        xs_ref[...] = jnp.zeros(xs_ref.shape, jnp.float32)

    a1, a2, a2s = a1_ref[...], a2_ref[...], a2s_ref[...]
    n_groups = a1.shape[0]
    pad = jnp.zeros((S5_GROUP_PITCH - n_groups, a1.shape[1]), jnp.float32)

    def body(c, carry):
        x, xs = carry
        row = pl.multiple_of(c * S5_GROUP_PITCH, 8)
        x0_ref[pl.ds(row, n_groups), :] = x
        x0_ref[pl.ds(row + n_groups, S5_GROUP_PITCH - n_groups), :] = pad
        return (a1 * x + a2 * xs + sa_ref[pl.ds(row, n_groups), :],
                a1 * xs + a2s * x + sb_ref[pl.ds(row, n_groups), :])

    n_chunks = sa_ref.shape[0] // S5_GROUP_PITCH
    x, xs = lax.fori_loop(0, n_chunks, body, (x_ref[...], xs_ref[...]), unroll=4)
    x_ref[...] = x
    xs_ref[...] = xs


def _s5_out_kernel(*refs, n_slabs, n_ch):
    u_refs = refs[:n_slabs]
    uf_ref, x0_ref, kt_ref, e_ref, d_ref, wglu_ref, bglu_ref, o_ref, t_ref = refs[n_slabs:n_slabs + 9]
    stage_refs, y_refs = refs[n_slabs + 9:2 * n_slabs + 9], refs[2 * n_slabs + 9:]
    n_groups, n_cb, cw = uf_ref.shape

    @pl.when(pl.program_id(0) == 0)
    def _():
        lane = lax.broadcasted_iota(jnp.int32, (n_ch, cw), 1)
        for g in range(n_groups):
            taps = kt_ref[g]
            for j in range(CHUNK):
                rows = taps if j == 0 else jnp.where(lane >= j * n_ch,
                                                     pltpu.roll(taps, j * n_ch, axis=1), 0.0)
                t_ref[g, j * n_ch:(j + 1) * n_ch, :] = rows.astype(t_ref.dtype)

    ys = [jnp.dot(uf_ref[g], t_ref[g], preferred_element_type=jnp.float32)
          + jnp.dot(x0_ref[pl.ds(g, n_cb, stride=S5_GROUP_PITCH), :].astype(jnp.bfloat16), e_ref[g],
                    preferred_element_type=jnp.float32)
          for g in range(n_groups)]
    _chunk_lanes_to_rows(ys, stage_refs, y_refs, n_cb, n_ch)
    y = jnp.concatenate([y_ref[...] for y_ref in y_refs], axis=1)
    u = jnp.concatenate([u_ref[...] for u_ref in u_refs], axis=1)
    y = jax.nn.gelu(y + d_ref[...] * u)
    gate = jnp.dot(y.astype(jnp.bfloat16), wglu_ref[...],
                   preferred_element_type=jnp.float32) + bglu_ref[...]
    o_ref[...] = (y * jax.nn.sigmoid(gate)).astype(o_ref.dtype)


def _s5_mixer(u, ktaps, wmat, emat, a1, a2, a2s, d, wglu, bglu):
    s_len, width = u.shape
    n_groups, n_ch, cw = ktaps.shape
    n_state = a1.shape[1] // 2
    n_chunks = s_len // CHUNK
    rows = S5_ROWS
    n_cb = rows // CHUNK
    lanes = V7X_LANES
    n_slabs = width // lanes
    params = pltpu.CompilerParams(dimension_semantics=("arbitrary",), vmem_limit_bytes=V7X_VMEM_LIMIT)
    slab_specs = [pl.BlockSpec((rows, lanes), functools.partial(lambda k, i: (i, k), k))
                  for k in range(n_slabs)]
    const = lambda a: pl.BlockSpec(a.shape, lambda i: (0,) * a.ndim, pipeline_mode=pl.Buffered(1))
    blk = lambda w: pl.BlockSpec((n_groups, n_cb, w), lambda i: (0, i, 0))
    assert n_groups <= S5_GROUP_PITCH and S5_GROUP_PITCH % 8 == 0
    cg_rows = pl.BlockSpec((n_cb * S5_GROUP_PITCH, 2 * n_state), lambda i: (i, 0))
    cg_shape = jax.ShapeDtypeStruct((n_chunks * S5_GROUP_PITCH, 2 * n_state), jnp.float32)
    u_flat, s_a, s_b = pl.pallas_call(
        functools.partial(_s5_state_kernel, n_slabs=n_slabs, n_ch=n_ch),
        grid=(s_len // rows,),
        in_specs=slab_specs + [const(wmat)],
        out_specs=[blk(cw), cg_rows, cg_rows],
        out_shape=[jax.ShapeDtypeStruct((n_groups, n_chunks, cw), jnp.bfloat16), cg_shape, cg_shape],
        compiler_params=params, name="s5_state",
    )(*([u] * n_slabs), wmat)
    cb = SCAN_BLOCK
    per_chunk = pl.BlockSpec((cb * S5_GROUP_PITCH, 2 * n_state), lambda i: (i, 0))
    x0 = pl.pallas_call(
        _s5_scan_kernel,
        grid=(n_chunks // cb,),
        in_specs=[per_chunk, per_chunk,
                  pl.BlockSpec(a1.shape, lambda i: (0, 0)),
                  pl.BlockSpec(a2.shape, lambda i: (0, 0)),
                  pl.BlockSpec(a2s.shape, lambda i: (0, 0))],
        out_specs=per_chunk,
        out_shape=cg_shape,
        scratch_shapes=[pltpu.VMEM(a1.shape, jnp.float32), pltpu.VMEM(a1.shape, jnp.float32)],
        compiler_params=params, name="s5_scan",
    )(s_a, s_b, a1, a2, a2s)
    return pl.pallas_call(
        functools.partial(_s5_out_kernel, n_slabs=n_slabs, n_ch=n_ch),
        grid=(s_len // rows,),
        in_specs=slab_specs + [blk(cw), cg_rows, const(ktaps), const(emat),
                               const(d), const(wglu), const(bglu)],
        out_specs=pl.BlockSpec((rows, width), lambda i: (i, 0)),
        out_shape=jax.ShapeDtypeStruct((s_len, width), jnp.bfloat16),
        scratch_shapes=([pltpu.VMEM((n_groups, cw, cw), jnp.bfloat16)]
                        + [pltpu.VMEM((rows + CHUNK * S5_STAGE_PAD, lanes), jnp.float32) for _ in range(n_slabs)]
                        + [pltpu.VMEM((rows, lanes), jnp.float32) for _ in range(n_slabs)]),
        compiler_params=params, name="s5_out",
    )(*([u] * n_slabs), u_flat, x0, ktaps, emat, d, wglu, bglu)


def _mix_ffn_kernel(x_ref, attn_ref, ssm_ref, woa_ref, wos_ref, g2_ref, wg_ref, wu_ref, wd_ref, gf_ref,
                    o_ref, act_ref, *, tf):
    mixed = (jnp.dot(attn_ref[...], woa_ref[...], preferred_element_type=jnp.float32)
             + jnp.dot(ssm_ref[...], wos_ref[...], preferred_element_type=jnp.float32))
    h = x_ref[...] + mixed
    hn = _rms(h, g2_ref[...]).astype(jnp.bfloat16)
    for f in range(0, wg_ref.shape[1], tf):
        gate = jnp.dot(hn, wg_ref[:, f:f + tf], preferred_element_type=jnp.float32)
        up = jnp.dot(hn, wu_ref[:, f:f + tf], preferred_element_type=jnp.float32)
        act_ref[:, f:f + tf] = (jax.nn.silu(gate) * up).astype(jnp.bfloat16)
    ffn = jnp.dot(act_ref[...], wd_ref[...], preferred_element_type=jnp.float32)
    o_ref[...] = _rms(h + ffn, gf_ref[...])


def _mix_ffn(x, attn, ssm, wo_attn, wo_ssm, g2, wg, wu, wd, gf):
    s_len, dm = x.shape
    bm = FFN_ROWS
    d_ff = wg.shape[1]
    row = lambda w: pl.BlockSpec((bm, w), lambda i: (i, 0))
    const = lambda a: pl.BlockSpec(a.shape, lambda i: (0, 0), pipeline_mode=pl.Buffered(1))
    return pl.pallas_call(
        functools.partial(_mix_ffn_kernel, tf=FFN_TILE),
        grid=(s_len // bm,),
        in_specs=[row(dm), row(attn.shape[1]), row(ssm.shape[1]),
                  const(wo_attn), const(wo_ssm), const(g2), const(wg), const(wu), const(wd), const(gf)],
        out_specs=row(dm),
        out_shape=jax.ShapeDtypeStruct((s_len, dm), jnp.float32),
        scratch_shapes=[pltpu.VMEM((bm, d_ff), jnp.bfloat16)],
        compiler_params=pltpu.CompilerParams(dimension_semantics=("arbitrary",),
                                             vmem_limit_bytes=V7X_VMEM_LIMIT),
        name="mix_ffn",
    )(x, attn, ssm, wo_attn, wo_ssm, g2, wg, wu, wd, gf)


def kernel(x, positions, norm1_g, w_in, lambda_q1, lambda_k1, lambda_q2, lambda_k2, subln_g, ssm_lambda_re, ssm_lambda_im, ssm_log_step, ssm_b_re, ssm_b_im, ssm_c_re, ssm_c_im, ssm_d, ssm_w_glu, ssm_b_glu, w_out, norm2_g, w_gate, w_up, w_down, final_g):
    bsz, s_len, d_model = x.shape
    assert bsz == 1 and norm1_g.shape[0] == 1, "one sequence, one layer"
    assert s_len % max(IN_PROJ_ROWS, FFN_ROWS, S5_ROWS, CHUNK * SCAN_BLOCK) == 0
    f32, bf16 = jnp.float32, jnp.bfloat16
    head_dim = lambda_q1.shape[-1]
    v_dim = subln_g.shape[-1]
    rot_dim = head_dim // 4
    n_groups, n_state, n_ch = ssm_b_re.shape[1:]
    ssm_width = n_groups * n_ch
    attn_width = d_model - ssm_width
    n_heads = attn_width // v_dim
    qk_width = n_heads * 2 * head_dim
    assert w_in.shape[-1] == 2 * qk_width + attn_width + ssm_width
    assert w_gate.shape[-1] % FFN_TILE == 0

    inv_freq = ROPE_THETA ** (-jnp.arange(0, rot_dim, 2, dtype=f32) / rot_dim)
    invf = jnp.concatenate([inv_freq, inv_freq]).reshape(rot_dim, 1)
    w = w_in[0]
    n_qkv = 2 * qk_width + attn_width
    qT, k, vT, u = _in_proj(x[0], positions[0], norm1_g[0].reshape(1, d_model).astype(f32), invf,
                            w[:, :n_qkv].T.astype(bf16), w[:, n_qkv:].astype(bf16),
                            n_heads=n_heads, head_dim=head_dim, rot_dim=rot_dim, v_dim=v_dim)

    lam_vec = lambda a: a[0].reshape(1, head_dim).astype(f32)
    attn = _attention(lam_vec(lambda_q1), lam_vec(lambda_k1), lam_vec(lambda_q2), lam_vec(lambda_k2),
                      subln_g[0].astype(f32), qT, k, vT, head_dim=head_dim)

    mats = _s5_matrices(ssm_lambda_re[0], ssm_lambda_im[0], ssm_log_step[0], ssm_b_re[0], ssm_b_im[0],
                        ssm_c_re[0], ssm_c_im[0])
    ssm = _s5_mixer(u, *mats, ssm_d[0].reshape(1, ssm_width).astype(f32), ssm_w_glu[0].astype(bf16),
                    ssm_b_glu[0].reshape(1, ssm_width).astype(f32))

    wo = w_out[0].astype(bf16)
    out = _mix_ffn(x[0], attn, ssm, wo[:attn_width], wo[attn_width:],
                   norm2_g[0].reshape(1, d_model).astype(f32),
                   w_gate[0].astype(bf16), w_up[0].astype(bf16), w_down[0].astype(bf16),
                   final_g.reshape(1, d_model).astype(f32))
    return out[None]
```

```python
import functools
import math

import jax
import jax.numpy as jnp
from jax import lax
from jax.experimental import pallas as pl
from jax.experimental.pallas import tpu as pltpu

NORM_EPS = 1e-5
ROPE_THETA = 500000.0
LAMBDA_INIT = 0.8 - 0.6 * math.exp(-0.3 * 0)
LAM_RE_MAX = -1e-4
LOG2_E = math.log2(math.e)
V_EXT_ROWS = 16
CHUNK = 16
V7X_VMEM_LIMIT = 56 * 1024 * 1024
V7X_LANES = 128

IN_PROJ_ROWS = 1024
ATTN_BQ = 1024
ATTN_HEADS_PER_STEP = 2
ATTN_TICKS_PER_LOOP = 2
ATTN_ROW_CHUNK = 64
ATTN_MAX_RISE = 100.0
S5_ROWS = 2048
S5_GROUP_PITCH = 40
S5_STAGE_PAD = 8
SCAN_BLOCK = 128
FFN_ROWS = 1024
FFN_TILE = 256

_NT = (((1,), (1,)), ((), ()))


def _rms(x, g):
    return x * lax.rsqrt(jnp.mean(x * x, axis=-1, keepdims=True) + NORM_EPS) * g


def _in_proj_kernel(x_ref, pos_ref, g_ref, invf_ref, wT_ref, wu_ref,
                    qT_ref, k_ref, vT_ref, u_ref, *, n_maps, head_dim, rot_dim, qk_width):
    hn = _rms(x_ref[...], g_ref[...]).astype(jnp.bfloat16)
    pT = lax.dot_general(wT_ref[...], hn, _NT, preferred_element_type=jnp.float32)
    u_ref[...] = jnp.dot(hn, wu_ref[...], preferred_element_type=jnp.float32)

    ang = invf_ref[...] * pos_ref[0].astype(jnp.float32)
    cos, sin = jnp.cos(ang), jnp.sin(ang)
    half = rot_dim // 2
    c0, c1, s0, s1 = cos[:half], cos[half:], sin[:half], sin[half:]

    def rope(t, scale):
        rows = []
        for m in range(n_maps):
            r = m * head_dim
            x0, x1 = t[r:r + half], t[r + half:r + rot_dim]
            rows += [x0 * c0 - x1 * s0, x1 * c1 + x0 * s1, t[r + rot_dim:r + head_dim]]
        out = jnp.concatenate(rows, axis=0)
        return out * scale if scale != 1.0 else out

    qT_ref[...] = rope(pT[:qk_width], head_dim ** -0.5 * LOG2_E).astype(jnp.bfloat16)
    n_heads, _, v_rows, bm = vT_ref.shape
    kT = rope(pT[qk_width:2 * qk_width], 1.0)
    for h in range(n_heads):
        k_ref[h] = kT[h * 2 * head_dim:(h + 1) * 2 * head_dim].T.astype(jnp.bfloat16)
    vT = pT[2 * qk_width:].astype(jnp.bfloat16).reshape(n_heads, v_rows - V_EXT_ROWS, bm)
    ext_row = lax.broadcasted_iota(jnp.int32, (n_heads, V_EXT_ROWS, bm), 1)
    ext = jnp.where(ext_row == 0, 1.0, 0.0).astype(jnp.bfloat16)
    vT_ref[...] = jnp.concatenate([vT, ext], axis=1).reshape(vT_ref.shape)


def _in_proj(x, pos, g, invf, wT, wu, *, n_heads, head_dim, rot_dim, v_dim):
    s_len, d = x.shape
    bm = IN_PROJ_ROWS
    qk_width = n_heads * 2 * head_dim
    v_width = n_heads * v_dim
    kern = functools.partial(_in_proj_kernel, n_maps=2 * n_heads, head_dim=head_dim,
                             rot_dim=rot_dim, qk_width=qk_width)
    const = lambda i: (0, 0)
    return pl.pallas_call(
        kern,
        grid=(s_len // bm,),
        in_specs=[pl.BlockSpec((bm, d), lambda i: (i, 0)),
                  pl.BlockSpec((1, 1, bm), lambda i: (i, 0, 0)),
                  pl.BlockSpec((1, d), const),
                  pl.BlockSpec((rot_dim, 1), const),
                  pl.BlockSpec(wT.shape, const),
                  pl.BlockSpec(wu.shape, const)],
        out_specs=[pl.BlockSpec((qk_width, bm), lambda i: (0, i)),
                   pl.BlockSpec((n_heads, bm, 2 * head_dim), lambda i: (0, i, 0)),
                   pl.BlockSpec((n_heads, 1, v_dim + V_EXT_ROWS, bm), lambda i: (0, i, 0, 0)),
                   pl.BlockSpec((bm, wu.shape[1]), lambda i: (i, 0))],
        out_shape=[jax.ShapeDtypeStruct((qk_width, s_len), jnp.bfloat16),
                   jax.ShapeDtypeStruct((n_heads, s_len, 2 * head_dim), jnp.bfloat16),
                   jax.ShapeDtypeStruct((n_heads, s_len // bm, v_dim + V_EXT_ROWS, bm), jnp.bfloat16),
                   jax.ShapeDtypeStruct((s_len, wu.shape[1]), jnp.float32)],
        compiler_params=pltpu.CompilerParams(dimension_semantics=("arbitrary",),
                                             vmem_limit_bytes=V7X_VMEM_LIMIT),
        name="in_proj",
    )(x, pos.reshape(s_len // bm, 1, bm), g, invf, wT, wu)


def _attn_kernel(lq1_ref, lk1_ref, lq2_ref, lk2_ref, g_ref, qT_ref, k_ref, vT_ref, o_ref, *scratch,
                 bq, bk, head_dim, v_dim, n_par):
    scratch = list(scratch)
    take = lambda n: [scratch.pop(0) for _ in range(n)]
    qbd_ref = take(n_par)
    p_ref = [take(n_par) for _ in range(2)]
    m_ref, acc_ref, rise_ref = take(n_par), take(n_par), take(n_par)

    i = pl.program_id(1)
    kw = 2 * head_dim
    n_full = (i * bq) // bk

    zero = jnp.zeros((head_dim, bq), qT_ref.dtype)
    for h in range(n_par):
        qT = qT_ref[h * kw:(h + 1) * kw, :]
        qbd_ref[h][...] = jnp.concatenate([jnp.concatenate([qT[:head_dim], zero], axis=1),
                                           jnp.concatenate([zero, qT[head_dim:]], axis=1)], axis=0)

    def scores(t, h, masked):
        kb = k_ref[h, pl.ds(pl.multiple_of(t * bk, bk), bk), :]
        s = jnp.dot(kb, qbd_ref[h][...], preferred_element_type=jnp.float32)
        if masked:
            kpos = t * bk + lax.broadcasted_iota(jnp.int32, s.shape, 0)
            qcol = lax.broadcasted_iota(jnp.int32, s.shape, 1)
            qpos = i * bq + jnp.where(qcol >= bq, qcol - bq, qcol)
            s = jnp.where(kpos <= qpos, s, -jnp.inf)
        return s

    def pv(t, h, slot):
        return jnp.dot(vT_ref[h, t], p_ref[slot][h][...], preferred_element_type=jnp.float32)

    def max_first_step(t, h, first):
        s = scores(t, h, masked=True)
        m_blk = jnp.max(s, axis=0, keepdims=True)
        m_new = m_blk if first else jnp.maximum(m_ref[h][...], m_blk)
        p_ref[0][h][...] = jnp.exp2(s - m_new).astype(p_ref[0][h].dtype)
        if first:
            acc_ref[h][...] = pv(t, h, 0)
        else:
            acc_ref[h][...] = jnp.exp2(m_ref[h][...] - m_new) * acc_ref[h][...] + pv(t, h, 0)
        m_ref[h][...] = m_new

    def stream_step(t, slot, masked):
        for h in range(n_par):
            r = m_ref[h][...]
            s = scores(t, h, masked)
            m8 = None
            for r0 in range(0, bk, ATTN_ROW_CHUNK):
                c = s[r0:r0 + ATTN_ROW_CHUNK]
                p_ref[slot][h][r0:r0 + ATTN_ROW_CHUNK, :] = jnp.exp2(c - r).astype(p_ref[slot][h].dtype)
                c8 = jnp.max(c.reshape(ATTN_ROW_CHUNK // 8, 8, c.shape[1]), axis=0)
                m8 = c8 if m8 is None else jnp.maximum(m8, c8)
            m_blk = jnp.max(m8, axis=0, keepdims=True)
            m_new = jnp.maximum(r, m_blk)
            rise_ref[h][...] = jnp.maximum(rise_ref[h][...], m_blk - r)
            m_ref[h][...] = m_new
            acc_ref[h][...] = jnp.exp2(r - m_new) * (acc_ref[h][...] + pv(t, h, slot))

    def finish():
        lam = (jnp.exp(jnp.sum(lq1_ref[...] * lk1_ref[...], axis=-1, keepdims=True))
               - jnp.exp(jnp.sum(lq2_ref[...] * lk2_ref[...], axis=-1, keepdims=True))
               + LAMBDA_INIT)
        for h in range(n_par):
            acc = acc_ref[h][...]
            o = acc[:v_dim] / acc[v_dim:v_dim + 1]
            oT = o[:, :bq] - lam * o[:, bq:]
            y = oT * lax.rsqrt(jnp.mean(oT * oT, axis=0, keepdims=True) + NORM_EPS) * g_ref[...]
            o_ref[:, h * v_dim:(h + 1) * v_dim] = (y * (1.0 - LAMBDA_INIT)).T.astype(o_ref.dtype)

    for h in range(n_par):
        max_first_step(0, h, first=True)
        rise_ref[h][...] = jnp.zeros(rise_ref[h].shape, jnp.float32)

    n_stream = jnp.maximum(n_full - 1, 0)

    def group(c, carry):
        t = 1 + ATTN_TICKS_PER_LOOP * c
        for r in range(ATTN_TICKS_PER_LOOP):
            stream_step(t + r, r % 2, masked=False)
        return carry

    lax.fori_loop(0, n_stream // ATTN_TICKS_PER_LOOP, group, 0)

    def single(t, carry):
        stream_step(t, 0, masked=False)
        return carry

    lax.fori_loop(1 + (n_stream // ATTN_TICKS_PER_LOOP) * ATTN_TICKS_PER_LOOP, n_full, single, 0)

    def stream_part(t, slot, key_off, key_len, q_off, q_len):
        lane_sets = [slice(c * bq + q_off, c * bq + q_off + q_len) for c in range(2)]
        part = lambda ref: jnp.concatenate([ref[:, ls] for ls in lane_sets], axis=1)
        for h in range(n_par):
            kb = k_ref[h, pl.ds(pl.multiple_of(t * bk + key_off, key_len), key_len), :]
            s = jnp.dot(kb, part(qbd_ref[h]), preferred_element_type=jnp.float32)
            kpos = t * bk + key_off + lax.broadcasted_iota(jnp.int32, s.shape, 0)
            qcol = lax.broadcasted_iota(jnp.int32, s.shape, 1)
            qpos = i * bq + q_off + jnp.where(qcol >= q_len, qcol - q_len, qcol)
            s = jnp.where(kpos <= qpos, s, -jnp.inf)
            r = part(m_ref[h])
            p_ref[slot][h][0:key_len, 0:2 * q_len] = jnp.exp2(s - r).astype(p_ref[slot][h].dtype)
            m_blk = jnp.max(s, axis=0, keepdims=True)
            m_new = jnp.maximum(r, m_blk)
            rise = jnp.maximum(part(rise_ref[h]), m_blk - r)
            pv = jnp.dot(vT_ref[h, t][:, key_off:key_off + key_len], p_ref[slot][h][0:key_len, 0:2 * q_len],
                         preferred_element_type=jnp.float32)
            acc = jnp.exp2(r - m_new) * (part(acc_ref[h]) + pv)
            for c, ls in enumerate(lane_sets):
                half = slice(c * q_len, (c + 1) * q_len)
                m_ref[h][:, ls] = m_new[:, half]
                rise_ref[h][:, ls] = rise[:, half]
                acc_ref[h][:, ls] = acc[:, half]

    n_diag = -(-bq // bk)
    if bq == bk:
        @pl.when(n_full >= 1)
        def _():
            stream_part(n_full, 1, 0, bk // 2, 0, bq)
            stream_part(n_full, 0, bk // 2, bk // 2, bq // 2, bq // 2)
    else:
        for d in range(n_diag):
            @pl.when(n_full + d >= 1)
            def _():
                stream_step(n_full + d, (d + 1) % 2, masked=True)

    finish()

    worst = rise_ref[0][...]
    for h in range(1, n_par):
        worst = jnp.maximum(worst, rise_ref[h][...])

    @pl.when(jnp.max(worst) > ATTN_MAX_RISE)
    def _():
        def redo(t, carry):
            for h in range(n_par):
                max_first_step(t, h, first=False)
            return carry

        for h in range(n_par):
            max_first_step(0, h, first=True)
        lax.fori_loop(1, n_full + n_diag, redo, 0)
        finish()


def _attention(lq1, lk1, lq2, lk2, subln_g, qT, k, vT, *, head_dim):
    n_heads, n_kblk, v_rows, bk = vT.shape
    v_dim = v_rows - V_EXT_ROWS
    s_len = k.shape[1]
    bq, n_par = ATTN_BQ, ATTN_HEADS_PER_STEP
    kern = functools.partial(_attn_kernel, bq=bq, bk=bk, head_dim=head_dim, v_dim=v_dim, n_par=n_par)
    vec = pl.BlockSpec((1, head_dim), lambda h, i: (0, 0))
    f32, bf16 = jnp.float32, jnp.bfloat16
    per_head = lambda shape, dtype, n=1: [pltpu.VMEM(shape, dtype) for _ in range(n * n_par)]
    scratch = (per_head((2 * head_dim, 2 * bq), bf16)
               + per_head((bk, 2 * bq), bf16, 2)
               + per_head((1, 2 * bq), f32)
               + per_head((v_rows, 2 * bq), f32)
               + per_head((1, 2 * bq), f32))
    resident = dict(pipeline_mode=pl.Buffered(1))
    return pl.pallas_call(
        kern,
        grid=(n_heads // n_par, s_len // bq),
        in_specs=[vec, vec, vec, vec,
                  pl.BlockSpec((v_dim, 1), lambda h, i: (0, 0)),
                  pl.BlockSpec((n_par * 2 * head_dim, bq), lambda h, i: (h, i)),
                  pl.BlockSpec((n_par, s_len, 2 * head_dim), lambda h, i: (h, 0, 0), **resident),
                  pl.BlockSpec((n_par, n_kblk, v_rows, bk), lambda h, i: (h, 0, 0, 0), **resident)],
        out_specs=pl.BlockSpec((bq, n_par * v_dim), lambda h, i: (i, h)),
        out_shape=jax.ShapeDtypeStruct((s_len, n_heads * v_dim), bf16),
        scratch_shapes=scratch,
        compiler_params=pltpu.CompilerParams(dimension_semantics=("arbitrary", "arbitrary"),
                                             vmem_limit_bytes=V7X_VMEM_LIMIT),
        name="attn",
    )(lq1, lk1, lq2, lk2, subln_g.reshape(v_dim, 1), qT, k, vT)


def _s5_matrices(lam_re, lam_im, log_step, b_re, b_im, c_re, c_im):
    f32 = jnp.float32
    n_groups, n_state, n_ch = b_re.shape
    lr = jnp.minimum(lam_re.astype(f32), LAM_RE_MAX)
    li = lam_im.astype(f32)
    step = jnp.exp(log_step.astype(f32))[:, None]
    mag = jnp.exp(lr * step)
    lb_re, lb_im = mag * jnp.cos(li * step), mag * jnp.sin(li * step)
    denom = lr * lr + li * li
    nr, ni = lb_re - 1.0, lb_im
    coef_re = (nr * lr + ni * li) / denom
    coef_im = (ni * lr - nr * li) / denom
    br, bi = b_re.astype(f32), b_im.astype(f32)
    bb_re = coef_re[..., None] * br - coef_im[..., None] * bi
    bb_im = coef_re[..., None] * bi + coef_im[..., None] * br
    tau = jnp.arange(CHUNK + 1, dtype=f32)[None, :, None]
    pmag = jnp.exp(lr[:, None, :] * step[:, None, :] * tau)
    pw_re = pmag * jnp.cos(li[:, None, :] * step[:, None, :] * tau)
    pw_im = pmag * jnp.sin(li[:, None, :] * step[:, None, :] * tau)
    cr, ci = c_re.astype(f32)[:, None], c_im.astype(f32)[:, None]
    cp_re = cr * pw_re[:, :, None, :] - ci * pw_im[:, :, None, :]
    cp_im = cr * pw_im[:, :, None, :] + ci * pw_re[:, :, None, :]
    tap_l = tuple(bb.transpose(0, 2, 1) for bb in (bb_re, bb_im))
    tap_r = tuple(cp[:, :CHUNK].reshape(n_groups, CHUNK * n_ch, n_state) for cp in (cp_re, cp_im))
    rev = CHUNK - 1 - jnp.arange(CHUNK)
    ar, ai = pw_re[:, rev][:, :, None, :], pw_im[:, rev][:, :, None, :]
    brt, bit = bb_re.transpose(0, 2, 1)[:, None], bb_im.transpose(0, 2, 1)[:, None]
    w_re = (ar * brt - ai * bit).reshape(n_groups, CHUNK * n_ch, n_state)
    w_im = (ar * bit + ai * brt).reshape(n_groups, CHUNK * n_ch, n_state)
    wmat = jnp.concatenate([w_re, w_im, w_im, w_re], axis=-1)
    e_re = cp_re[:, 1:].transpose(0, 3, 1, 2).reshape(n_groups, n_state, CHUNK * n_ch)
    e_im = cp_im[:, 1:].transpose(0, 3, 1, 2).reshape(n_groups, n_state, CHUNK * n_ch)
    emat = jnp.concatenate([e_re, -e_im], axis=1)
    al_re, al_im = pw_re[:, CHUNK], pw_im[:, CHUNK]
    a1 = jnp.concatenate([al_re, al_re], axis=-1)
    a2 = jnp.concatenate([-al_im, al_im], axis=-1)
    bf = jnp.bfloat16
    return tap_l + tap_r, wmat.astype(bf), emat.astype(bf), a1, a2, -a2


def _transpose_pieces(arrs, width):
    n = len(arrs)
    piece = lax.broadcasted_iota(jnp.int32, arrs[0].shape, 1) // width
    arrs = list(arrs)
    d = n // 2
    while d:
        upper = (piece // d) % 2 == 1
        for a in range(n):
            if a & d == 0:
                lo, hi = arrs[a], arrs[a + d]
                arrs[a] = jnp.where(upper, pltpu.roll(hi, d * width, axis=1), lo)
                arrs[a + d] = jnp.where(upper, hi, pltpu.roll(lo, (n - d) * width, axis=1))
        d //= 2
    return arrs


def _chunk_rows_to_lanes(u_refs, n_cb, n_ch):
    gps = u_refs[0].shape[1] // n_ch
    out = []
    for u_ref in u_refs:
        rows = [u_ref[pl.ds(j, n_cb, stride=CHUNK), :] for j in range(CHUNK)]
        cols = [_transpose_pieces(rows[m * gps:(m + 1) * gps], n_ch) for m in range(CHUNK // gps)]
        out += [jnp.concatenate([c[gl] for c in cols], axis=1) for gl in range(gps)]
    return out


def _chunk_lanes_to_rows(ys, stage_refs, o_refs, n_cb, n_ch):
    lanes = o_refs[0].shape[1]
    gps = lanes // n_ch
    pitch = stage_refs[0].shape[0] // CHUNK
    for slab, stage_ref in enumerate(stage_refs):
        for m in range(CHUNK // gps):
            group_cols = [ys[slab * gps + gl][:, m * lanes:(m + 1) * lanes] for gl in range(gps)]
            for r, x in enumerate(_transpose_pieces(group_cols, n_ch)):
                t = m * gps + r
                stage_ref[t * pitch:t * pitch + n_cb, :] = x

    def chunk_rows(c, carry):
        for stage_ref, o_ref in zip(stage_refs, o_refs):
            o_ref[pl.ds(pl.multiple_of(c * CHUNK, CHUNK), CHUNK), :] = stage_ref[pl.ds(c, CHUNK, stride=pitch), :]
        return carry

    lax.fori_loop(0, n_cb, chunk_rows, 0, unroll=8)


def _s5_state_kernel(*refs, n_slabs, n_ch):
    u_refs, (w_ref, uf_ref, sa_ref, sb_ref) = refs[:n_slabs], refs[n_slabs:]
    flat = _chunk_rows_to_lanes(u_refs, uf_ref.shape[1], n_ch)
    n_groups, n_cb, _ = uf_ref.shape
    half = sa_ref.shape[1]
    for g, f in enumerate(flat):
        f = f.astype(jnp.bfloat16)
        uf_ref[g] = f
        s = jnp.dot(f, w_ref[g], preferred_element_type=jnp.float32)
        sa_ref[pl.ds(g, n_cb, stride=S5_GROUP_PITCH), :] = s[:, :half]
        sb_ref[pl.ds(g, n_cb, stride=S5_GROUP_PITCH), :] = s[:, half:]
    for g in range(n_groups, S5_GROUP_PITCH):
        sa_ref[pl.ds(g, n_cb, stride=S5_GROUP_PITCH), :] = jnp.zeros((n_cb, half), jnp.float32)
        sb_ref[pl.ds(g, n_cb, stride=S5_GROUP_PITCH), :] = jnp.zeros((n_cb, half), jnp.float32)


def _s5_scan_kernel(sa_ref, sb_ref, a1_ref, a2_ref, a2s_ref, x0_ref, x_ref, xs_ref):
    @pl.when(pl.program_id(0) == 0)
    def _():
        x_ref[...] = jnp.zeros(x_ref.shape, jnp.float32)
        xs_ref[...] = jnp.zeros(xs_ref.shape, jnp.float32)

    a1, a2, a2s = a1_ref[...], a2_ref[...], a2s_ref[...]
    n_groups = a1.shape[0]
    pad = jnp.zeros((S5_GROUP_PITCH - n_groups, a1.shape[1]), jnp.float32)

    def body(c, carry):
        x, xs = carry
        row = pl.multiple_of(c * S5_GROUP_PITCH, 8)
        x0_ref[pl.ds(row, n_groups), :] = x
        x0_ref[pl.ds(row + n_groups, S5_GROUP_PITCH - n_groups), :] = pad
        return (a1 * x + a2 * xs + sa_ref[pl.ds(row, n_groups), :],
                a1 * xs + a2s * x + sb_ref[pl.ds(row, n_groups), :])

    n_chunks = sa_ref.shape[0] // S5_GROUP_PITCH
    x, xs = lax.fori_loop(0, n_chunks, body, (x_ref[...], xs_ref[...]), unroll=4)
    x_ref[...] = x
    xs_ref[...] = xs


def _s5_out_kernel(*refs, n_slabs, n_ch):
    u_refs = refs[:n_slabs]
    (uf_ref, x0_ref, bre_ref, bim_ref, cre_ref, cim_ref, e_ref, d_ref, wglu_ref, bglu_ref,
     o_ref, t_ref) = refs[n_slabs:n_slabs + 12]
    stage_refs, y_refs = refs[n_slabs + 12:2 * n_slabs + 12], refs[2 * n_slabs + 12:]
    n_groups, n_cb, cw = uf_ref.shape

    @pl.when(pl.program_id(0) == 0)
    def _():
        lane = lax.broadcasted_iota(jnp.int32, (n_ch, cw), 1)
        for g in range(n_groups):
            taps = (lax.dot_general(bre_ref[g], cre_ref[g], _NT, precision=lax.Precision.HIGHEST,
                                    preferred_element_type=jnp.float32)
                    - lax.dot_general(bim_ref[g], cim_ref[g], _NT, precision=lax.Precision.HIGHEST,
                                      preferred_element_type=jnp.float32))
            for j in range(CHUNK):
                rows = taps if j == 0 else jnp.where(lane >= j * n_ch,
                                                     pltpu.roll(taps, j * n_ch, axis=1), 0.0)
                t_ref[g, j * n_ch:(j + 1) * n_ch, :] = rows.astype(t_ref.dtype)

    ys = [jnp.dot(uf_ref[g], t_ref[g], preferred_element_type=jnp.float32)
          + jnp.dot(x0_ref[pl.ds(g, n_cb, stride=S5_GROUP_PITCH), :].astype(jnp.bfloat16), e_ref[g],
                    preferred_element_type=jnp.float32)
          for g in range(n_groups)]
    _chunk_lanes_to_rows(ys, stage_refs, y_refs, n_cb, n_ch)
    y = jnp.concatenate([y_ref[...] for y_ref in y_refs], axis=1)
    u = jnp.concatenate([u_ref[...] for u_ref in u_refs], axis=1)
    y = jax.nn.gelu(y + d_ref[...] * u)
    gate = jnp.dot(y.astype(jnp.bfloat16), wglu_ref[...],
                   preferred_element_type=jnp.float32) + bglu_ref[...]
    o_ref[...] = (y * jax.nn.sigmoid(gate)).astype(o_ref.dtype)


def _s5_mixer(u, tap_factors, wmat, emat, a1, a2, a2s, d, wglu, bglu):
    s_len, width = u.shape
    n_groups, n_ch, _ = tap_factors[0].shape
    cw = CHUNK * n_ch
    n_state = a1.shape[1] // 2
    n_chunks = s_len // CHUNK
    rows = S5_ROWS
    n_cb = rows // CHUNK
    lanes = V7X_LANES
    n_slabs = width // lanes
    params = pltpu.CompilerParams(dimension_semantics=("arbitrary",), vmem_limit_bytes=V7X_VMEM_LIMIT)
    slab_specs = [pl.BlockSpec((rows, lanes), functools.partial(lambda k, i: (i, k), k))
                  for k in range(n_slabs)]
    const = lambda a: pl.BlockSpec(a.shape, lambda i: (0,) * a.ndim, pipeline_mode=pl.Buffered(1))
    blk = lambda w: pl.BlockSpec((n_groups, n_cb, w), lambda i: (0, i, 0))
    assert n_groups <= S5_GROUP_PITCH and S5_GROUP_PITCH % 8 == 0
    cg_rows = pl.BlockSpec((n_cb * S5_GROUP_PITCH, 2 * n_state), lambda i: (i, 0))
    cg_shape = jax.ShapeDtypeStruct((n_chunks * S5_GROUP_PITCH, 2 * n_state), jnp.float32)
    u_flat, s_a, s_b = pl.pallas_call(
        functools.partial(_s5_state_kernel, n_slabs=n_slabs, n_ch=n_ch),
        grid=(s_len // rows,),
        in_specs=slab_specs + [const(wmat)],
        out_specs=[blk(cw), cg_rows, cg_rows],
        out_shape=[jax.ShapeDtypeStruct((n_groups, n_chunks, cw), jnp.bfloat16), cg_shape, cg_shape],
        compiler_params=params, name="s5_state",
    )(*([u] * n_slabs), wmat)
    cb = SCAN_BLOCK
    per_chunk = pl.BlockSpec((cb * S5_GROUP_PITCH, 2 * n_state), lambda i: (i, 0))
    x0 = pl.pallas_call(
        _s5_scan_kernel,
        grid=(n_chunks // cb,),
        in_specs=[per_chunk, per_chunk,
                  pl.BlockSpec(a1.shape, lambda i: (0, 0)),
                  pl.BlockSpec(a2.shape, lambda i: (0, 0)),
                  pl.BlockSpec(a2s.shape, lambda i: (0, 0))],
        out_specs=per_chunk,
        out_shape=cg_shape,
        scratch_shapes=[pltpu.VMEM(a1.shape, jnp.float32), pltpu.VMEM(a1.shape, jnp.float32)],
        compiler_params=params, name="s5_scan",
    )(s_a, s_b, a1, a2, a2s)
    return pl.pallas_call(
        functools.partial(_s5_out_kernel, n_slabs=n_slabs, n_ch=n_ch),
        grid=(s_len // rows,),
        in_specs=slab_specs + [blk(cw), cg_rows, *map(const, tap_factors), const(emat),
                               const(d), const(wglu), const(bglu)],
        out_specs=pl.BlockSpec((rows, width), lambda i: (i, 0)),
        out_shape=jax.ShapeDtypeStruct((s_len, width), jnp.bfloat16),
        scratch_shapes=([pltpu.VMEM((n_groups, cw, cw), jnp.bfloat16)]
                        + [pltpu.VMEM((rows + CHUNK * S5_STAGE_PAD, lanes), jnp.float32) for _ in range(n_slabs)]
                        + [pltpu.VMEM((rows, lanes), jnp.float32) for _ in range(n_slabs)]),
        compiler_params=params, name="s5_out",
    )(*([u] * n_slabs), u_flat, x0, *tap_factors, emat, d, wglu, bglu)


def _mix_ffn_kernel(x_ref, attn_ref, ssm_ref, woa_ref, wos_ref, g2_ref, wg_ref, wu_ref, wd_ref, gf_ref,
                    o_ref, act_ref, *, tf):
    mixed = (jnp.dot(attn_ref[...], woa_ref[...], preferred_element_type=jnp.float32)
             + jnp.dot(ssm_ref[...], wos_ref[...], preferred_element_type=jnp.float32))
    h = x_ref[...] + mixed
    hn = _rms(h, g2_ref[...]).astype(jnp.bfloat16)
    for f in range(0, wg_ref.shape[1], tf):
        gate = jnp.dot(hn, wg_ref[:, f:f + tf], preferred_element_type=jnp.float32)
        up = jnp.dot(hn, wu_ref[:, f:f + tf], preferred_element_type=jnp.float32)
        act_ref[:, f:f + tf] = (jax.nn.silu(gate) * up).astype(jnp.bfloat16)
    ffn = jnp.dot(act_ref[...], wd_ref[...], preferred_element_type=jnp.float32)
    o_ref[...] = _rms(h + ffn, gf_ref[...])


def _mix_ffn(x, attn, ssm, wo_attn, wo_ssm, g2, wg, wu, wd, gf):
    s_len, dm = x.shape
    bm = FFN_ROWS
    d_ff = wg.shape[1]
    row = lambda w: pl.BlockSpec((bm, w), lambda i: (i, 0))
    const = lambda a: pl.BlockSpec(a.shape, lambda i: (0, 0), pipeline_mode=pl.Buffered(1))
    return pl.pallas_call(
        functools.partial(_mix_ffn_kernel, tf=FFN_TILE),
        grid=(s_len // bm,),
        in_specs=[row(dm), row(attn.shape[1]), row(ssm.shape[1]),
                  const(wo_attn), const(wo_ssm), const(g2), const(wg), const(wu), const(wd), const(gf)],
        out_specs=row(dm),
        out_shape=jax.ShapeDtypeStruct((s_len, dm), jnp.float32),
        scratch_shapes=[pltpu.VMEM((bm, d_ff), jnp.bfloat16)],
        compiler_params=pltpu.CompilerParams(dimension_semantics=("arbitrary",),
                                             vmem_limit_bytes=V7X_VMEM_LIMIT),
        name="mix_ffn",
    )(x, attn, ssm, wo_attn, wo_ssm, g2, wg, wu, wd, gf)


def kernel(x, positions, norm1_g, w_in, lambda_q1, lambda_k1, lambda_q2, lambda_k2, subln_g, ssm_lambda_re, ssm_lambda_im, ssm_log_step, ssm_b_re, ssm_b_im, ssm_c_re, ssm_c_im, ssm_d, ssm_w_glu, ssm_b_glu, w_out, norm2_g, w_gate, w_up, w_down, final_g):
    bsz, s_len, d_model = x.shape
    assert bsz == 1 and norm1_g.shape[0] == 1, "one sequence, one layer"
    assert s_len % max(IN_PROJ_ROWS, FFN_ROWS, S5_ROWS, CHUNK * SCAN_BLOCK) == 0
    f32, bf16 = jnp.float32, jnp.bfloat16
    head_dim = lambda_q1.shape[-1]
    v_dim = subln_g.shape[-1]
    rot_dim = head_dim // 4
    n_groups, n_state, n_ch = ssm_b_re.shape[1:]
    ssm_width = n_groups * n_ch
    attn_width = d_model - ssm_width
    n_heads = attn_width // v_dim
    qk_width = n_heads * 2 * head_dim
    assert w_in.shape[-1] == 2 * qk_width + attn_width + ssm_width
    assert w_gate.shape[-1] % FFN_TILE == 0

    inv_freq = ROPE_THETA ** (-jnp.arange(0, rot_dim, 2, dtype=f32) / rot_dim)
    invf = jnp.concatenate([inv_freq, inv_freq]).reshape(rot_dim, 1)
    w = w_in[0]
    n_qkv = 2 * qk_width + attn_width
    qT, k, vT, u = _in_proj(x[0], positions[0], norm1_g[0].reshape(1, d_model).astype(f32), invf,
                            w[:, :n_qkv].T.astype(bf16), w[:, n_qkv:].astype(bf16),
                            n_heads=n_heads, head_dim=head_dim, rot_dim=rot_dim, v_dim=v_dim)

    lam_vec = lambda a: a[0].reshape(1, head_dim).astype(f32)
    attn = _attention(lam_vec(lambda_q1), lam_vec(lambda_k1), lam_vec(lambda_q2), lam_vec(lambda_k2),
                      subln_g[0].astype(f32), qT, k, vT, head_dim=head_dim)

    mats = _s5_matrices(ssm_lambda_re[0], ssm_lambda_im[0], ssm_log_step[0], ssm_b_re[0], ssm_b_im[0],
                        ssm_c_re[0], ssm_c_im[0])
    ssm = _s5_mixer(u, *mats, ssm_d[0].reshape(1, ssm_width).astype(f32), ssm_w_glu[0].astype(bf16),
                    ssm_b_glu[0].reshape(1, ssm_width).astype(f32))

    wo = w_out[0].astype(bf16)
    out = _mix_ffn(x[0], attn, ssm, wo[:attn_width], wo[attn_width:],
                   norm2_g[0].reshape(1, d_model).astype(f32),
                   w_gate[0].astype(bf16), w_up[0].astype(bf16), w_down[0].astype(bf16),
                   final_g.reshape(1, d_model).astype(f32))
    return out[None]
```

```python
import functools
import math

import jax
import jax.numpy as jnp
from jax import lax
from jax.experimental import pallas as pl
from jax.experimental.pallas import tpu as pltpu

NORM_EPS = 1e-5
ROPE_THETA = 500000.0
LAMBDA_INIT = 0.8 - 0.6 * math.exp(-0.3 * 0)
LAM_RE_MAX = -1e-4
LOG2_E = math.log2(math.e)
V_EXT_ROWS = 16
CHUNK = 16
V7X_VMEM_LIMIT = 56 * 1024 * 1024
V7X_LANES = 128

IN_PROJ_ROWS = 1024
ATTN_BQ = 1024
ATTN_HEADS_PER_STEP = 2
ATTN_TICKS_PER_LOOP = 2
ATTN_ROW_CHUNK = 64
ATTN_MAX_RISE = 100.0
S5_ROWS = 2048
S5_GROUP_PITCH = 40
S5_STAGE_PAD = 8
FFN_ROWS = 1024
FFN_TILE = 256

_NT = (((1,), (1,)), ((), ()))


def _rms(x, g):
    return x * lax.rsqrt(jnp.mean(x * x, axis=-1, keepdims=True) + NORM_EPS) * g


def _in_proj_kernel(x_ref, pos_ref, g_ref, invf_ref, wT_ref, wu_ref,
                    qT_ref, k_ref, vT_ref, u_ref, *, n_maps, head_dim, rot_dim, qk_width):
    hn = _rms(x_ref[...], g_ref[...]).astype(jnp.bfloat16)
    pT = lax.dot_general(wT_ref[...], hn, _NT, preferred_element_type=jnp.float32)
    u_ref[...] = jnp.dot(hn, wu_ref[...], preferred_element_type=jnp.float32)

    ang = invf_ref[...] * pos_ref[0].astype(jnp.float32)
    cos, sin = jnp.cos(ang), jnp.sin(ang)
    half = rot_dim // 2
    c0, c1, s0, s1 = cos[:half], cos[half:], sin[:half], sin[half:]

    def rope(t, scale):
        rows = []
        for m in range(n_maps):
            r = m * head_dim
            x0, x1 = t[r:r + half], t[r + half:r + rot_dim]
            rows += [x0 * c0 - x1 * s0, x1 * c1 + x0 * s1, t[r + rot_dim:r + head_dim]]
        out = jnp.concatenate(rows, axis=0)
        return out * scale if scale != 1.0 else out

    qT_ref[...] = rope(pT[:qk_width], head_dim ** -0.5 * LOG2_E).astype(jnp.bfloat16)
    n_heads, _, v_rows, bm = vT_ref.shape
    kT = rope(pT[qk_width:2 * qk_width], 1.0)
    for h in range(n_heads):
        k_ref[h] = kT[h * 2 * head_dim:(h + 1) * 2 * head_dim].T.astype(jnp.bfloat16)
    vT = pT[2 * qk_width:].astype(jnp.bfloat16).reshape(n_heads, v_rows - V_EXT_ROWS, bm)
    ext_row = lax.broadcasted_iota(jnp.int32, (n_heads, V_EXT_ROWS, bm), 1)
    ext = jnp.where(ext_row == 0, 1.0, 0.0).astype(jnp.bfloat16)
    vT_ref[...] = jnp.concatenate([vT, ext], axis=1).reshape(vT_ref.shape)


def _in_proj(x, pos, g, invf, wT, wu, *, n_heads, head_dim, rot_dim, v_dim):
    s_len, d = x.shape
    bm = IN_PROJ_ROWS
    qk_width = n_heads * 2 * head_dim
    v_width = n_heads * v_dim
    kern = functools.partial(_in_proj_kernel, n_maps=2 * n_heads, head_dim=head_dim,
                             rot_dim=rot_dim, qk_width=qk_width)
    const = lambda i: (0, 0)
    return pl.pallas_call(
        kern,
        grid=(s_len // bm,),
        in_specs=[pl.BlockSpec((bm, d), lambda i: (i, 0)),
                  pl.BlockSpec((1, 1, bm), lambda i: (i, 0, 0)),
                  pl.BlockSpec((1, d), const),
                  pl.BlockSpec((rot_dim, 1), const),
                  pl.BlockSpec(wT.shape, const),
                  pl.BlockSpec(wu.shape, const)],
        out_specs=[pl.BlockSpec((qk_width, bm), lambda i: (0, i)),
                   pl.BlockSpec((n_heads, bm, 2 * head_dim), lambda i: (0, i, 0)),
                   pl.BlockSpec((n_heads, 1, v_dim + V_EXT_ROWS, bm), lambda i: (0, i, 0, 0)),
                   pl.BlockSpec((bm, wu.shape[1]), lambda i: (i, 0))],
        out_shape=[jax.ShapeDtypeStruct((qk_width, s_len), jnp.bfloat16),
                   jax.ShapeDtypeStruct((n_heads, s_len, 2 * head_dim), jnp.bfloat16),
                   jax.ShapeDtypeStruct((n_heads, s_len // bm, v_dim + V_EXT_ROWS, bm), jnp.bfloat16),
                   jax.ShapeDtypeStruct((s_len, wu.shape[1]), jnp.float32)],
        compiler_params=pltpu.CompilerParams(dimension_semantics=("arbitrary",),
                                             vmem_limit_bytes=V7X_VMEM_LIMIT),
        name="in_proj",
    )(x, pos.reshape(s_len // bm, 1, bm), g, invf, wT, wu)


def _attn_kernel(lq1_ref, lk1_ref, lq2_ref, lk2_ref, g_ref, qT_ref, k_ref, vT_ref, o_ref, *scratch,
                 bq, bk, head_dim, v_dim, n_par):
    scratch = list(scratch)
    take = lambda n: [scratch.pop(0) for _ in range(n)]
    qbd_ref = take(n_par)
    p_ref = [take(n_par) for _ in range(2)]
    m_ref, acc_ref, rise_ref = take(n_par), take(n_par), take(n_par)

    i = pl.program_id(1)
    kw = 2 * head_dim
    n_full = (i * bq) // bk

    zero = jnp.zeros((head_dim, bq), qT_ref.dtype)
    for h in range(n_par):
        qT = qT_ref[h * kw:(h + 1) * kw, :]
        qbd_ref[h][...] = jnp.concatenate([jnp.concatenate([qT[:head_dim], zero], axis=1),
                                           jnp.concatenate([zero, qT[head_dim:]], axis=1)], axis=0)

    def scores(t, h, masked):
        kb = k_ref[h, pl.ds(pl.multiple_of(t * bk, bk), bk), :]
        s = jnp.dot(kb, qbd_ref[h][...], preferred_element_type=jnp.float32)
        if masked:
            kpos = t * bk + lax.broadcasted_iota(jnp.int32, s.shape, 0)
            qcol = lax.broadcasted_iota(jnp.int32, s.shape, 1)
            qpos = i * bq + jnp.where(qcol >= bq, qcol - bq, qcol)
            s = jnp.where(kpos <= qpos, s, -jnp.inf)
        return s

    def pv(t, h, slot):
        return jnp.dot(vT_ref[h, t], p_ref[slot][h][...], preferred_element_type=jnp.float32)

    def max_first_step(t, h, first):
        s = scores(t, h, masked=True)
        m_blk = jnp.max(s, axis=0, keepdims=True)
        m_new = m_blk if first else jnp.maximum(m_ref[h][...], m_blk)
        p_ref[0][h][...] = jnp.exp2(s - m_new).astype(p_ref[0][h].dtype)
        if first:
            acc_ref[h][...] = pv(t, h, 0)
        else:
            acc_ref[h][...] = jnp.exp2(m_ref[h][...] - m_new) * acc_ref[h][...] + pv(t, h, 0)
        m_ref[h][...] = m_new

    def stream_step(t, slot, masked):
        for h in range(n_par):
            r = m_ref[h][...]
            s = scores(t, h, masked)
            m8 = None
            for r0 in range(0, bk, ATTN_ROW_CHUNK):
                c = s[r0:r0 + ATTN_ROW_CHUNK]
                p_ref[slot][h][r0:r0 + ATTN_ROW_CHUNK, :] = jnp.exp2(c - r).astype(p_ref[slot][h].dtype)
                c8 = jnp.max(c.reshape(ATTN_ROW_CHUNK // 8, 8, c.shape[1]), axis=0)
                m8 = c8 if m8 is None else jnp.maximum(m8, c8)
            m_blk = jnp.max(m8, axis=0, keepdims=True)
            m_new = jnp.maximum(r, m_blk)
            rise_ref[h][...] = jnp.maximum(rise_ref[h][...], m_blk - r)
            m_ref[h][...] = m_new
            acc_ref[h][...] = jnp.exp2(r - m_new) * (acc_ref[h][...] + pv(t, h, slot))

    def finish():
        lam = (jnp.exp(jnp.sum(lq1_ref[...] * lk1_ref[...], axis=-1, keepdims=True))
               - jnp.exp(jnp.sum(lq2_ref[...] * lk2_ref[...], axis=-1, keepdims=True))
               + LAMBDA_INIT)
        for h in range(n_par):
            acc = acc_ref[h][...]
            o = acc[:v_dim] / acc[v_dim:v_dim + 1]
            oT = o[:, :bq] - lam * o[:, bq:]
            y = oT * lax.rsqrt(jnp.mean(oT * oT, axis=0, keepdims=True) + NORM_EPS) * g_ref[...]
            o_ref[:, h * v_dim:(h + 1) * v_dim] = (y * (1.0 - LAMBDA_INIT)).T.astype(o_ref.dtype)

    for h in range(n_par):
        max_first_step(0, h, first=True)
        rise_ref[h][...] = jnp.zeros(rise_ref[h].shape, jnp.float32)

    n_stream = jnp.maximum(n_full - 1, 0)

    def group(c, carry):
        t = 1 + ATTN_TICKS_PER_LOOP * c
        for r in range(ATTN_TICKS_PER_LOOP):
            stream_step(t + r, r % 2, masked=False)
        return carry

    lax.fori_loop(0, n_stream // ATTN_TICKS_PER_LOOP, group, 0)

    def single(t, carry):
        stream_step(t, 0, masked=False)
        return carry

    lax.fori_loop(1 + (n_stream // ATTN_TICKS_PER_LOOP) * ATTN_TICKS_PER_LOOP, n_full, single, 0)

    def stream_part(t, slot, key_off, key_len, q_off, q_len):
        lane_sets = [slice(c * bq + q_off, c * bq + q_off + q_len) for c in range(2)]
        part = lambda ref: jnp.concatenate([ref[:, ls] for ls in lane_sets], axis=1)
        for h in range(n_par):
            kb = k_ref[h, pl.ds(pl.multiple_of(t * bk + key_off, key_len), key_len), :]
            s = jnp.dot(kb, part(qbd_ref[h]), preferred_element_type=jnp.float32)
            kpos = t * bk + key_off + lax.broadcasted_iota(jnp.int32, s.shape, 0)
            qcol = lax.broadcasted_iota(jnp.int32, s.shape, 1)
            qpos = i * bq + q_off + jnp.where(qcol >= q_len, qcol - q_len, qcol)
            s = jnp.where(kpos <= qpos, s, -jnp.inf)
            r = part(m_ref[h])
            p_ref[slot][h][0:key_len, 0:2 * q_len] = jnp.exp2(s - r).astype(p_ref[slot][h].dtype)
            m_blk = jnp.max(s, axis=0, keepdims=True)
            m_new = jnp.maximum(r, m_blk)
            rise = jnp.maximum(part(rise_ref[h]), m_blk - r)
            pv = jnp.dot(vT_ref[h, t][:, key_off:key_off + key_len], p_ref[slot][h][0:key_len, 0:2 * q_len],
                         preferred_element_type=jnp.float32)
            acc = jnp.exp2(r - m_new) * (part(acc_ref[h]) + pv)
            for c, ls in enumerate(lane_sets):
                half = slice(c * q_len, (c + 1) * q_len)
                m_ref[h][:, ls] = m_new[:, half]
                rise_ref[h][:, ls] = rise[:, half]
                acc_ref[h][:, ls] = acc[:, half]

    n_diag = -(-bq // bk)
    if bq == bk:
        @pl.when(n_full >= 1)
        def _():
            stream_part(n_full, 1, 0, bk // 2, 0, bq)
            stream_part(n_full, 0, bk // 2, bk // 2, bq // 2, bq // 2)
    else:
        for d in range(n_diag):
            @pl.when(n_full + d >= 1)
            def _():
                stream_step(n_full + d, (d + 1) % 2, masked=True)

    finish()

    worst = rise_ref[0][...]
    for h in range(1, n_par):
        worst = jnp.maximum(worst, rise_ref[h][...])

    @pl.when(jnp.max(worst) > ATTN_MAX_RISE)
    def _():
        def redo(t, carry):
            for h in range(n_par):
                max_first_step(t, h, first=False)
            return carry

        for h in range(n_par):
            max_first_step(0, h, first=True)
        lax.fori_loop(1, n_full + n_diag, redo, 0)
        finish()


def _attention(lq1, lk1, lq2, lk2, subln_g, qT, k, vT, *, head_dim):
    n_heads, n_kblk, v_rows, bk = vT.shape
    v_dim = v_rows - V_EXT_ROWS
    s_len = k.shape[1]
    bq, n_par = ATTN_BQ, ATTN_HEADS_PER_STEP
    kern = functools.partial(_attn_kernel, bq=bq, bk=bk, head_dim=head_dim, v_dim=v_dim, n_par=n_par)
    vec = pl.BlockSpec((1, head_dim), lambda h, i: (0, 0))
    f32, bf16 = jnp.float32, jnp.bfloat16
    per_head = lambda shape, dtype, n=1: [pltpu.VMEM(shape, dtype) for _ in range(n * n_par)]
    scratch = (per_head((2 * head_dim, 2 * bq), bf16)
               + per_head((bk, 2 * bq), bf16, 2)
               + per_head((1, 2 * bq), f32)
               + per_head((v_rows, 2 * bq), f32)
               + per_head((1, 2 * bq), f32))
    resident = dict(pipeline_mode=pl.Buffered(1))
    return pl.pallas_call(
        kern,
        grid=(n_heads // n_par, s_len // bq),
        in_specs=[vec, vec, vec, vec,
                  pl.BlockSpec((v_dim, 1), lambda h, i: (0, 0)),
                  pl.BlockSpec((n_par * 2 * head_dim, bq), lambda h, i: (h, i)),
                  pl.BlockSpec((n_par, s_len, 2 * head_dim), lambda h, i: (h, 0, 0), **resident),
                  pl.BlockSpec((n_par, n_kblk, v_rows, bk), lambda h, i: (h, 0, 0, 0), **resident)],
        out_specs=pl.BlockSpec((bq, n_par * v_dim), lambda h, i: (i, h)),
        out_shape=jax.ShapeDtypeStruct((s_len, n_heads * v_dim), bf16),
        scratch_shapes=scratch,
        compiler_params=pltpu.CompilerParams(dimension_semantics=("arbitrary", "arbitrary"),
                                             vmem_limit_bytes=V7X_VMEM_LIMIT),
        name="attn",
    )(lq1, lk1, lq2, lk2, subln_g.reshape(v_dim, 1), qT, k, vT)


def _s5_matrices(lam_re, lam_im, log_step, b_re, b_im, c_re, c_im):
    f32 = jnp.float32
    n_groups, n_state, n_ch = b_re.shape
    lr = jnp.minimum(lam_re.astype(f32), LAM_RE_MAX)
    li = lam_im.astype(f32)
    step = jnp.exp(log_step.astype(f32))[:, None]
    mag = jnp.exp(lr * step)
    lb_re, lb_im = mag * jnp.cos(li * step), mag * jnp.sin(li * step)
    denom = lr * lr + li * li
    nr, ni = lb_re - 1.0, lb_im
    coef_re = (nr * lr + ni * li) / denom
    coef_im = (ni * lr - nr * li) / denom
    br, bi = b_re.astype(f32), b_im.astype(f32)
    bb_re = coef_re[..., None] * br - coef_im[..., None] * bi
    bb_im = coef_re[..., None] * bi + coef_im[..., None] * br
    tau = jnp.arange(CHUNK + 1, dtype=f32)[None, :, None]
    pmag = jnp.exp(lr[:, None, :] * step[:, None, :] * tau)
    pw_re = pmag * jnp.cos(li[:, None, :] * step[:, None, :] * tau)
    pw_im = pmag * jnp.sin(li[:, None, :] * step[:, None, :] * tau)
    cr, ci = c_re.astype(f32)[:, None], c_im.astype(f32)[:, None]
    cp_re = cr * pw_re[:, :, None, :] - ci * pw_im[:, :, None, :]
    cp_im = cr * pw_im[:, :, None, :] + ci * pw_re[:, :, None, :]
    tap_l = tuple(bb.transpose(0, 2, 1) for bb in (bb_re, bb_im))
    tap_r = tuple(cp[:, :CHUNK].reshape(n_groups, CHUNK * n_ch, n_state) for cp in (cp_re, cp_im))
    rev = CHUNK - 1 - jnp.arange(CHUNK)
    ar, ai = pw_re[:, rev][:, :, None, :], pw_im[:, rev][:, :, None, :]
    brt, bit = bb_re.transpose(0, 2, 1)[:, None], bb_im.transpose(0, 2, 1)[:, None]
    w_re = (ar * brt - ai * bit).reshape(n_groups, CHUNK * n_ch, n_state)
    w_im = (ar * bit + ai * brt).reshape(n_groups, CHUNK * n_ch, n_state)
    wmat = jnp.concatenate([w_re, w_im, w_im, w_re], axis=-1)
    e_re = cp_re[:, 1:].transpose(0, 3, 1, 2).reshape(n_groups, n_state, CHUNK * n_ch)
    e_im = cp_im[:, 1:].transpose(0, 3, 1, 2).reshape(n_groups, n_state, CHUNK * n_ch)
    emat = jnp.concatenate([e_re, -e_im], axis=1)
    al_re, al_im = pw_re[:, CHUNK], pw_im[:, CHUNK]
    a1 = jnp.concatenate([al_re, al_re], axis=-1)
    a2 = jnp.concatenate([-al_im, al_im], axis=-1)
    bf = jnp.bfloat16
    return tap_l + tap_r, wmat.astype(bf), emat.astype(bf), a1, a2, -a2


def _transpose_pieces(arrs, width):
    n = len(arrs)
    piece = lax.broadcasted_iota(jnp.int32, arrs[0].shape, 1) // width
    arrs = list(arrs)
    d = n // 2
    while d:
        upper = (piece // d) % 2 == 1
        for a in range(n):
            if a & d == 0:
                lo, hi = arrs[a], arrs[a + d]
                arrs[a] = jnp.where(upper, pltpu.roll(hi, d * width, axis=1), lo)
                arrs[a + d] = jnp.where(upper, hi, pltpu.roll(lo, (n - d) * width, axis=1))
        d //= 2
    return arrs


def _chunk_rows_to_lanes(u_refs, n_cb, n_ch):
    gps = u_refs[0].shape[1] // n_ch
    out = []
    for u_ref in u_refs:
        rows = [u_ref[pl.ds(j, n_cb, stride=CHUNK), :] for j in range(CHUNK)]
        cols = [_transpose_pieces(rows[m * gps:(m + 1) * gps], n_ch) for m in range(CHUNK // gps)]
        out += [jnp.concatenate([c[gl] for c in cols], axis=1) for gl in range(gps)]
    return out


def _chunk_lanes_to_rows(ys, stage_refs, o_refs, n_cb, n_ch):
    lanes = o_refs[0].shape[1]
    gps = lanes // n_ch
    pitch = stage_refs[0].shape[0] // CHUNK
    for slab, stage_ref in enumerate(stage_refs):
        for m in range(CHUNK // gps):
            group_cols = [ys[slab * gps + gl][:, m * lanes:(m + 1) * lanes] for gl in range(gps)]
            for r, x in enumerate(_transpose_pieces(group_cols, n_ch)):
                t = m * gps + r
                stage_ref[t * pitch:t * pitch + n_cb, :] = x

    def chunk_rows(c, carry):
        for stage_ref, o_ref in zip(stage_refs, o_refs):
            o_ref[pl.ds(pl.multiple_of(c * CHUNK, CHUNK), CHUNK), :] = stage_ref[pl.ds(c, CHUNK, stride=pitch), :]
        return carry

    lax.fori_loop(0, n_cb, chunk_rows, 0, unroll=8)


def _s5_state_kernel(*refs, n_slabs, n_ch):
    u_refs = refs[:n_slabs]
    w_ref, a1_ref, a2_ref, a2s_ref, uf_ref, x0_ref, sa_ref, sb_ref, x_ref, xs_ref = refs[n_slabs:]
    flat = _chunk_rows_to_lanes(u_refs, uf_ref.shape[1], n_ch)
    n_groups, n_cb, _ = uf_ref.shape
    half = sa_ref.shape[1]
    for g, f in enumerate(flat):
        f = f.astype(jnp.bfloat16)
        uf_ref[g] = f
        s = jnp.dot(f, w_ref[g], preferred_element_type=jnp.float32)
        sa_ref[pl.ds(g, n_cb, stride=S5_GROUP_PITCH), :] = s[:, :half]
        sb_ref[pl.ds(g, n_cb, stride=S5_GROUP_PITCH), :] = s[:, half:]
    _s5_scan(sa_ref, sb_ref, a1_ref, a2_ref, a2s_ref, x0_ref, x_ref, xs_ref)


def _s5_scan(sa_ref, sb_ref, a1_ref, a2_ref, a2s_ref, x0_ref, x_ref, xs_ref):
    @pl.when(pl.program_id(0) == 0)
    def _():
        x_ref[...] = jnp.zeros(x_ref.shape, jnp.float32)
        xs_ref[...] = jnp.zeros(xs_ref.shape, jnp.float32)

    a1, a2, a2s = a1_ref[...], a2_ref[...], a2s_ref[...]
    n_groups = a1.shape[0]
    pad = jnp.zeros((S5_GROUP_PITCH - n_groups, a1.shape[1]), jnp.float32)

    def body(c, carry):
        x, xs = carry
        row = pl.multiple_of(c * S5_GROUP_PITCH, 8)
        x0_ref[pl.ds(row, n_groups), :] = x
        x0_ref[pl.ds(row + n_groups, S5_GROUP_PITCH - n_groups), :] = pad
        return (a1 * x + a2 * xs + sa_ref[pl.ds(row, n_groups), :],
                a1 * xs + a2s * x + sb_ref[pl.ds(row, n_groups), :])

    n_chunks = sa_ref.shape[0] // S5_GROUP_PITCH
    x, xs = lax.fori_loop(0, n_chunks, body, (x_ref[...], xs_ref[...]), unroll=4)
    x_ref[...] = x
    xs_ref[...] = xs


def _s5_out_kernel(*refs, n_slabs, n_ch):
    u_refs = refs[:n_slabs]
    (uf_ref, x0_ref, bre_ref, bim_ref, cre_ref, cim_ref, e_ref, d_ref, wglu_ref, bglu_ref,
     o_ref, t_ref) = refs[n_slabs:n_slabs + 12]
    stage_refs, y_refs = refs[n_slabs + 12:2 * n_slabs + 12], refs[2 * n_slabs + 12:]
    n_groups, n_cb, cw = uf_ref.shape

    @pl.when(pl.program_id(0) == 0)
    def _():
        lane = lax.broadcasted_iota(jnp.int32, (n_ch, cw), 1)
        for g in range(n_groups):
            taps = (lax.dot_general(bre_ref[g], cre_ref[g], _NT, precision=lax.Precision.HIGHEST,
                                    preferred_element_type=jnp.float32)
                    - lax.dot_general(bim_ref[g], cim_ref[g], _NT, precision=lax.Precision.HIGHEST,
                                      preferred_element_type=jnp.float32))
            for j in range(CHUNK):
                rows = taps if j == 0 else jnp.where(lane >= j * n_ch,
                                                     pltpu.roll(taps, j * n_ch, axis=1), 0.0)
                t_ref[g, j * n_ch:(j + 1) * n_ch, :] = rows.astype(t_ref.dtype)

    ys = [jnp.dot(uf_ref[g], t_ref[g], preferred_element_type=jnp.float32)
          + jnp.dot(x0_ref[pl.ds(g, n_cb, stride=S5_GROUP_PITCH), :].astype(jnp.bfloat16), e_ref[g],
                    preferred_element_type=jnp.float32)
          for g in range(n_groups)]
    _chunk_lanes_to_rows(ys, stage_refs, y_refs, n_cb, n_ch)
    y = jnp.concatenate([y_ref[...] for y_ref in y_refs], axis=1)
    u = jnp.concatenate([u_ref[...] for u_ref in u_refs], axis=1)
    y = jax.nn.gelu(y + d_ref[...] * u)
    gate = jnp.dot(y.astype(jnp.bfloat16), wglu_ref[...],
                   preferred_element_type=jnp.float32) + bglu_ref[...]
    o_ref[...] = (y * jax.nn.sigmoid(gate)).astype(o_ref.dtype)


def _s5_mixer(u, tap_factors, wmat, emat, a1, a2, a2s, d, wglu, bglu):
    s_len, width = u.shape
    n_groups, n_ch, _ = tap_factors[0].shape
    cw = CHUNK * n_ch
    n_state = a1.shape[1] // 2
    n_chunks = s_len // CHUNK
    rows = S5_ROWS
    n_cb = rows // CHUNK
    lanes = V7X_LANES
    n_slabs = width // lanes
    params = pltpu.CompilerParams(dimension_semantics=("arbitrary",), vmem_limit_bytes=V7X_VMEM_LIMIT)
    slab_specs = [pl.BlockSpec((rows, lanes), functools.partial(lambda k, i: (i, k), k))
                  for k in range(n_slabs)]
    const = lambda a: pl.BlockSpec(a.shape, lambda i: (0,) * a.ndim, pipeline_mode=pl.Buffered(1))
    blk = lambda w: pl.BlockSpec((n_groups, n_cb, w), lambda i: (0, i, 0))
    assert n_groups <= S5_GROUP_PITCH and S5_GROUP_PITCH % 8 == 0
    cg_rows = pl.BlockSpec((n_cb * S5_GROUP_PITCH, 2 * n_state), lambda i: (i, 0))
    cg_shape = jax.ShapeDtypeStruct((n_chunks * S5_GROUP_PITCH, 2 * n_state), jnp.float32)
    u_flat, x0 = pl.pallas_call(
        functools.partial(_s5_state_kernel, n_slabs=n_slabs, n_ch=n_ch),
        grid=(s_len // rows,),
        in_specs=slab_specs + [const(wmat), const(a1), const(a2), const(a2s)],
        out_specs=[blk(cw), cg_rows],
        out_shape=[jax.ShapeDtypeStruct((n_groups, n_chunks, cw), jnp.bfloat16), cg_shape],
        scratch_shapes=[pltpu.VMEM(cg_rows.block_shape, jnp.float32), pltpu.VMEM(cg_rows.block_shape, jnp.float32),
                        pltpu.VMEM(a1.shape, jnp.float32), pltpu.VMEM(a1.shape, jnp.float32)],
        compiler_params=params, name="s5_state",
    )(*([u] * n_slabs), wmat, a1, a2, a2s)
    return pl.pallas_call(
        functools.partial(_s5_out_kernel, n_slabs=n_slabs, n_ch=n_ch),
        grid=(s_len // rows,),
        in_specs=slab_specs + [blk(cw), cg_rows, *map(const, tap_factors), const(emat),
                               const(d), const(wglu), const(bglu)],
        out_specs=pl.BlockSpec((rows, width), lambda i: (i, 0)),
        out_shape=jax.ShapeDtypeStruct((s_len, width), jnp.bfloat16),
        scratch_shapes=([pltpu.VMEM((n_groups, cw, cw), jnp.bfloat16)]
                        + [pltpu.VMEM((rows + CHUNK * S5_STAGE_PAD, lanes), jnp.float32) for _ in range(n_slabs)]
                        + [pltpu.VMEM((rows, lanes), jnp.float32) for _ in range(n_slabs)]),
        compiler_params=params, name="s5_out",
    )(*([u] * n_slabs), u_flat, x0, *tap_factors, emat, d, wglu, bglu)


def _mix_ffn_kernel(x_ref, attn_ref, ssm_ref, woa_ref, wos_ref, g2_ref, wg_ref, wu_ref, wd_ref, gf_ref,
                    o_ref, act_ref, *, tf):
    mixed = (jnp.dot(attn_ref[...], woa_ref[...], preferred_element_type=jnp.float32)
             + jnp.dot(ssm_ref[...], wos_ref[...], preferred_element_type=jnp.float32))
    h = x_ref[...] + mixed
    hn = _rms(h, g2_ref[...]).astype(jnp.bfloat16)
    for f in range(0, wg_ref.shape[1], tf):
        gate = jnp.dot(hn, wg_ref[:, f:f + tf], preferred_element_type=jnp.float32)
        up = jnp.dot(hn, wu_ref[:, f:f + tf], preferred_element_type=jnp.float32)
        act_ref[:, f:f + tf] = (jax.nn.silu(gate) * up).astype(jnp.bfloat16)
    ffn = jnp.dot(act_ref[...], wd_ref[...], preferred_element_type=jnp.float32)
    o_ref[...] = _rms(h + ffn, gf_ref[...])


def _mix_ffn(x, attn, ssm, wo_attn, wo_ssm, g2, wg, wu, wd, gf):
    s_len, dm = x.shape
    bm = FFN_ROWS
    d_ff = wg.shape[1]
    row = lambda w: pl.BlockSpec((bm, w), lambda i: (i, 0))
    const = lambda a: pl.BlockSpec(a.shape, lambda i: (0, 0), pipeline_mode=pl.Buffered(1))
    return pl.pallas_call(
        functools.partial(_mix_ffn_kernel, tf=FFN_TILE),
        grid=(s_len // bm,),
        in_specs=[row(dm), row(attn.shape[1]), row(ssm.shape[1]),
                  const(wo_attn), const(wo_ssm), const(g2), const(wg), const(wu), const(wd), const(gf)],
        out_specs=row(dm),
        out_shape=jax.ShapeDtypeStruct((s_len, dm), jnp.float32),
        scratch_shapes=[pltpu.VMEM((bm, d_ff), jnp.bfloat16)],
        compiler_params=pltpu.CompilerParams(dimension_semantics=("arbitrary",),
                                             vmem_limit_bytes=V7X_VMEM_LIMIT),
        name="mix_ffn",
    )(x, attn, ssm, wo_attn, wo_ssm, g2, wg, wu, wd, gf)


def kernel(x, positions, norm1_g, w_in, lambda_q1, lambda_k1, lambda_q2, lambda_k2, subln_g, ssm_lambda_re, ssm_lambda_im, ssm_log_step, ssm_b_re, ssm_b_im, ssm_c_re, ssm_c_im, ssm_d, ssm_w_glu, ssm_b_glu, w_out, norm2_g, w_gate, w_up, w_down, final_g):
    bsz, s_len, d_model = x.shape
    assert bsz == 1 and norm1_g.shape[0] == 1, "one sequence, one layer"
    assert s_len % max(IN_PROJ_ROWS, FFN_ROWS, S5_ROWS) == 0
    f32, bf16 = jnp.float32, jnp.bfloat16
    head_dim = lambda_q1.shape[-1]
    v_dim = subln_g.shape[-1]
    rot_dim = head_dim // 4
    n_groups, n_state, n_ch = ssm_b_re.shape[1:]
    ssm_width = n_groups * n_ch
    attn_width = d_model - ssm_width
    n_heads = attn_width // v_dim
    qk_width = n_heads * 2 * head_dim
    assert w_in.shape[-1] == 2 * qk_width + attn_width + ssm_width
    assert w_gate.shape[-1] % FFN_TILE == 0

    inv_freq = ROPE_THETA ** (-jnp.arange(0, rot_dim, 2, dtype=f32) / rot_dim)
    invf = jnp.concatenate([inv_freq, inv_freq]).reshape(rot_dim, 1)
    w = w_in[0]
    n_qkv = 2 * qk_width + attn_width
    qT, k, vT, u = _in_proj(x[0], positions[0], norm1_g[0].reshape(1, d_model).astype(f32), invf,
                            w[:, :n_qkv].T.astype(bf16), w[:, n_qkv:].astype(bf16),
                            n_heads=n_heads, head_dim=head_dim, rot_dim=rot_dim, v_dim=v_dim)

    lam_vec = lambda a: a[0].reshape(1, head_dim).astype(f32)
    attn = _attention(lam_vec(lambda_q1), lam_vec(lambda_k1), lam_vec(lambda_q2), lam_vec(lambda_k2),
                      subln_g[0].astype(f32), qT, k, vT, head_dim=head_dim)

    mats = _s5_matrices(ssm_lambda_re[0], ssm_lambda_im[0], ssm_log_step[0], ssm_b_re[0], ssm_b_im[0],
                        ssm_c_re[0], ssm_c_im[0])
    ssm = _s5_mixer(u, *mats, ssm_d[0].reshape(1, ssm_width).astype(f32), ssm_w_glu[0].astype(bf16),
                    ssm_b_glu[0].reshape(1, ssm_width).astype(f32))

    wo = w_out[0].astype(bf16)
    out = _mix_ffn(x[0], attn, ssm, wo[:attn_width], wo[attn_width:],
                   norm2_g[0].reshape(1, d_model).astype(f32),
                   w_gate[0].astype(bf16), w_up[0].astype(bf16), w_down[0].astype(bf16),
                   final_g.reshape(1, d_model).astype(f32))
    return out[None]
```
